```python
import jax, jax.numpy as jnp
from jax import lax
import numpy as np

D_MODEL = 1024
BATCH = 8
SEQ = 2048
DEPTH = 1
DEC_BATCH = 128
DEC_SEQ = 8
PAST_LEN = 8192
PAGE_SIZE = 128

N_HEADS = 8
HEAD_DIM = 64
D_ATTN = N_HEADS * HEAD_DIM
D_RNN = D_MODEL - D_ATTN
N_RNN_BLOCKS = 8
RNN_BLOCK = D_RNN // N_RNN_BLOCKS
CONV_WIDTH = 4
LRU_C = 8.0
MOBA_BLOCK = 256
MOBA_TOPK = 3
Q_CHUNK = 16
PEER_HEADS = 8
PEER_NKEYS = 128
PEER_DKEY = 256
PEER_TOPK = 16
PEER_CHUNK = 128
N_EXPERTS = PEER_NKEYS * PEER_NKEYS
D_IN = 3 * D_ATTN + 2 * D_RNN
N_MOD = 6
EPS = 1e-6

kernel_name = "hymba_moba_rglru_peer_step"


def rms_norm(x, gain):
    xf = x.astype(jnp.float32)
    y = xf * lax.rsqrt(jnp.mean(xf * xf, axis=-1, keepdims=True) + EPS)
    return (y * gain.astype(jnp.float32)).astype(x.dtype)


def alibi_slopes():
    return 2.0 ** (-8.0 * jnp.arange(1, N_HEADS + 1, dtype=jnp.float32) / N_HEADS)


def _scores(q, k):
    if k.ndim == 4:
        return jnp.einsum('bqhd,bkhd->bqhk', q, k)
    return jnp.einsum('bqhd,bqhkd->bqhk', q, k)


def _weighted_values(p, v):
    if v.ndim == 4:
        return jnp.einsum('bqhk,bkhd->bqhd', p, v)
    return jnp.einsum('bqhk,bqhkd->bqhd', p, v)


def alibi_softmax_attend(q, parts):
    slopes = alibi_slopes()[:, None]
    logits = []
    for k, _, dist, ok in parts:
        s = _scores(q, k).astype(jnp.float32) * (HEAD_DIM ** -0.5) - slopes * dist.astype(jnp.float32)
        logits.append(jnp.where(ok, s, -jnp.inf))
    p = jax.nn.softmax(jnp.concatenate(logits, axis=-1), axis=-1).astype(q.dtype)
    out, start = None, 0
    for (_, v, _, _), s in zip(parts, logits):
        n = s.shape[-1]
        o = _weighted_values(p[..., start:start + n], v)
        out = o if out is None else out + o
        start += n
    return out


def moba_prompt(q, k, v):
    B, S, H, Dh = q.shape
    nb = -(-S // MOBA_BLOCK)
    pad = nb * MOBA_BLOCK - S
    kb = jnp.pad(k, ((0, 0), (0, pad), (0, 0), (0, 0))).reshape(B, nb, MOBA_BLOCK, H, Dh)
    vb = jnp.pad(v, ((0, 0), (0, pad), (0, 0), (0, 0))).reshape(B, nb, MOBA_BLOCK, H, Dh)
    n_sel = min(MOBA_TOPK, nb - 1)
    n_chunks = S // Q_CHUNK
    to_chunks = lambda a: a.reshape((B, n_chunks, Q_CHUNK) + a.shape[2:]).swapaxes(0, 1)
    xs = (jnp.arange(n_chunks), to_chunks(q))
    if n_sel > 0:
        qblk = jnp.arange(S) // MOBA_BLOCK
        kmean = jnp.mean(kb, axis=2, dtype=jnp.float32)
        gate = jnp.einsum('bshd,bnhd->bshn', q.astype(jnp.float32), kmean)
        past = jnp.arange(nb)[None, :] < qblk[:, None]
        gate = jnp.where(past[None, :, None, :], gate, -jnp.inf)
        _, sel = lax.top_k(gate, n_sel)
        ok = sel < qblk[None, :, None, None]
        xs = xs + (to_chunks(sel), to_chunks(ok))
        kbt = kb.transpose(0, 3, 1, 2, 4)
        vbt = vb.transpose(0, 3, 1, 2, 4)
    bi = jnp.arange(B)[:, None, None, None]
    hi = jnp.arange(H)[None, None, :, None]

    def one_chunk(args):
        ci, q_c = args[0], args[1]
        t = ci * Q_CHUNK + jnp.arange(Q_CHUNK)
        own = (ci * Q_CHUNK) // MOBA_BLOCK
        s_own = own * MOBA_BLOCK + jnp.arange(MOBA_BLOCK)
        dist_own = (t[:, None] - s_own[None, :])[:, None, :]
        k_own = lax.dynamic_index_in_dim(kb, own, axis=1, keepdims=False)
        v_own = lax.dynamic_index_in_dim(vb, own, axis=1, keepdims=False)
        parts = [(k_own, v_own, dist_own, dist_own >= 0)]
        if n_sel > 0:
            sel_c, ok_c = args[2], args[3]
            k_sel = kbt[bi, hi, sel_c].reshape(B, Q_CHUNK, H, n_sel * MOBA_BLOCK, Dh)
            v_sel = vbt[bi, hi, sel_c].reshape(B, Q_CHUNK, H, n_sel * MOBA_BLOCK, Dh)
            s_sel = (sel_c[..., None] * MOBA_BLOCK + jnp.arange(MOBA_BLOCK)).reshape(B, Q_CHUNK, H, -1)
            ok_sel = jnp.repeat(ok_c, MOBA_BLOCK, axis=-1)
            parts.append((k_sel, v_sel, t[None, :, None, None] - s_sel, ok_sel))
        return alibi_softmax_attend(q_c, parts)

    out = lax.map(one_chunk, xs)
    return out.swapaxes(0, 1).reshape(B, S, H, Dh)


def moba_sample(q, k_new, v_new, cache_k, cache_v, layer, page_sum_k, page_table):
    B, T, H, Dh = q.shape
    n_full = PAST_LEN // MOBA_BLOCK
    own_start = n_full * MOBA_BLOCK
    own_len = PAST_LEN - own_start
    n_sel = min(MOBA_TOPK, n_full)
    t_new = PAST_LEN + jnp.arange(T)
    xs = (t_new, q.swapaxes(0, 1)[:, :, None])
    if n_sel > 0:
        ppb = MOBA_BLOCK // PAGE_SIZE
        kmean = page_sum_k[page_table[:, :n_full * ppb]].reshape(B, n_full, ppb, H, Dh).sum(axis=2) / MOBA_BLOCK
        gate = jnp.einsum('bthd,bnhd->bthn', q.astype(jnp.float32), kmean)
        _, sel = lax.top_k(gate, n_sel)
        xs = xs + (sel.swapaxes(0, 1)[:, :, None],)
    if own_len > 0:
        s_op = own_start + jnp.arange(own_len)
        phys_op = page_table[:, s_op // PAGE_SIZE]
        k_op = cache_k[layer, phys_op, s_op % PAGE_SIZE]
        v_op = cache_v[layer, phys_op, s_op % PAGE_SIZE]
    b5 = jnp.arange(B)[:, None, None, None, None]
    h5 = jnp.arange(H)[None, None, :, None, None]

    def one_query(args):
        t, q_t = args[0], args[1]
        parts = []
        if n_sel > 0:
            pos = args[2][..., None] * MOBA_BLOCK + jnp.arange(MOBA_BLOCK)
            phys = page_table[b5, pos // PAGE_SIZE]
            off = pos % PAGE_SIZE
            k_sel = cache_k[layer, phys, off, h5].reshape(B, 1, H, n_sel * MOBA_BLOCK, Dh)
            v_sel = cache_v[layer, phys, off, h5].reshape(B, 1, H, n_sel * MOBA_BLOCK, Dh)
            parts.append((k_sel, v_sel, t - pos.reshape(B, 1, H, -1), True))
        if own_len > 0:
            parts.append((k_op, v_op, (t - s_op)[None, None, :], True))
        dist_new = (t - t_new)[None, None, :]
        parts.append((k_new, v_new, dist_new, dist_new >= 0))
        return alibi_softmax_attend(q_t, parts)

    out = lax.map(one_query, xs)
    return out[:, :, 0].swapaxes(0, 1)


def causal_conv(x, buf, w, b):
    T = x.shape[1]
    xx = jnp.concatenate([buf.astype(x.dtype), x], axis=1)
    y = b + xx[:, 0:T] * w[0]
    for j in range(1, CONV_WIDTH):
        y = y + xx[:, j:j + T] * w[j]
    return y, xx[:, -(CONV_WIDTH - 1):]


def rg_lru(x, h0, w_a, b_a, w_x, b_x, lam):
    B, T, _ = x.shape
    xf = x.astype(jnp.float32)
    xb = xf.reshape(B, T, N_RNN_BLOCKS, RNN_BLOCK)
    r = jax.nn.sigmoid(jnp.einsum('btnc,ncd->btnd', xb, w_a.astype(jnp.float32)) + b_a).reshape(B, T, D_RNN)
    i = jax.nn.sigmoid(jnp.einsum('btnc,ncd->btnd', xb, w_x.astype(jnp.float32)) + b_x).reshape(B, T, D_RNN)
    log_a = -LRU_C * r * jax.nn.softplus(-lam.astype(jnp.float32))
    a = jnp.exp(log_a)
    u = jnp.sqrt(-jnp.expm1(2.0 * log_a)) * (i * xf)
    u = u.at[:, 0].add(a[:, 0] * h0.astype(jnp.float32))

    def combine(left, right):
        a1, b1 = left
        a2, b2 = right
        return a1 * a2, a2 * b1 + b2

    _, h = lax.associative_scan(combine, (a, u), axis=1)
    return h.astype(x.dtype), h[:, -1].astype(x.dtype)


def peer(x, w_q, sub_keys, u, v):
    B, T, D = x.shape
    n = B * T
    n_pad = -(-n // PEER_CHUNK) * PEER_CHUNK
    xt = jnp.pad(x.reshape(n, D), ((0, n_pad - n), (0, 0))).reshape(n_pad // PEER_CHUNK, PEER_CHUNK, D)
    half = PEER_DKEY // 2

    def chunk(xc):
        q = (xc @ w_q).reshape(PEER_CHUNK, PEER_HEADS, 2, half)
        s = jnp.einsum('chpk,hpnk->chpn', q, sub_keys).astype(jnp.float32)
        sv, si = lax.top_k(s, PEER_TOPK)
        cand = (sv[:, :, 0, :, None] + sv[:, :, 1, None, :]).reshape(PEER_CHUNK, PEER_HEADS, -1)
        cidx = (si[:, :, 0, :, None] * PEER_NKEYS + si[:, :, 1, None, :]).reshape(PEER_CHUNK, PEER_HEADS, -1)
        fv, fi = lax.top_k(cand, PEER_TOPK)
        e = jnp.take_along_axis(cidx, fi, axis=-1)
        g = jax.nn.softmax(fv, axis=-1).astype(xc.dtype)
        act = jax.nn.gelu(jnp.einsum('cd,chkd->chk', xc, u[e]))
        return jnp.einsum('chk,chkd->cd', g * act, v[e])

    y = lax.map(chunk, xt).reshape(n_pad, D)[:n]
    return y.reshape(B, T, D)


def decoder_layer(x, c, attend, conv_buf, h0, w_ada, b_ada, norm_mix, norm_ff, w_in, q_gain, k_gain,
                  conv_w, conv_b, lru_wa, lru_ba, lru_wx, lru_bx, lru_lambda, attn_out_gain,
                  rnn_out_gain, w_out, peer_wq, peer_keys, peer_u, peer_v):
    B, T, _ = x.shape
    mod = (jax.nn.silu(c) @ w_ada + b_ada)[:, None, :]
    shift_m, scale_m, gate_m, shift_f, scale_f, gate_f = jnp.split(mod, N_MOD, axis=-1)
    h = rms_norm(x, norm_mix) * (1 + scale_m) + shift_m
    proj = h @ w_in
    q, k, v, xr, gr = jnp.split(proj, [D_ATTN, 2 * D_ATTN, 3 * D_ATTN, 3 * D_ATTN + D_RNN], axis=-1)
    q = rms_norm(q.reshape(B, T, N_HEADS, HEAD_DIM), q_gain)
    k = rms_norm(k.reshape(B, T, N_HEADS, HEAD_DIM), k_gain)
    v = v.reshape(B, T, N_HEADS, HEAD_DIM)
    attn = attend(q, k, v).reshape(B, T, D_ATTN)
    xc, new_buf = causal_conv(xr, conv_buf, conv_w, conv_b)
    hr, h_last = rg_lru(xc, h0, lru_wa, lru_ba, lru_wx, lru_bx, lru_lambda)
    rnn = hr * jax.nn.gelu(gr)
    mixed = jnp.concatenate([rms_norm(attn, attn_out_gain), rms_norm(rnn, rnn_out_gain)], axis=-1) @ w_out
    x = x + gate_m * mixed
    h = rms_norm(x, norm_ff) * (1 + scale_f) + shift_f
    x = x + gate_f * peer(h, peer_wq, peer_keys, peer_u, peer_v)
    return x, k, v, new_buf, h_last


def setup_inputs(seed: int = 0) -> dict:
    key = jax.random.key(seed)
    keys = jax.random.split(key, 40)
    f32 = jnp.float32

    def nrm(i, shape, scale):
        return jax.random.normal(keys[i], shape, f32) * scale

    n_pages = PAST_LEN // PAGE_SIZE
    n_used = DEC_BATCH * n_pages
    n_pool = n_used + (n_used + 3) // 4
    page_table = jax.random.permutation(keys[6], n_pool)[:n_used].reshape(DEC_BATCH, n_pages).astype(jnp.int32)
    a_c = jax.random.uniform(keys[7], (DEPTH, D_RNN), f32, 0.9, 0.999)
    sig = a_c ** (1.0 / LRU_C)
    lru_lambda = jnp.log(sig) - jnp.log1p(-sig)
    sd = D_MODEL ** -0.5
    return {
        'x_prompt': nrm(0, (BATCH, SEQ, D_MODEL), 1.0),
        'x_sample': nrm(1, (DEC_BATCH, DEC_SEQ, D_MODEL), 1.0),
        'cache_k': nrm(2, (DEPTH, n_pool, PAGE_SIZE, N_HEADS, HEAD_DIM), 1.0),
        'cache_v': nrm(3, (DEPTH, n_pool, PAGE_SIZE, N_HEADS, HEAD_DIM), 1.0),
        'state_conv': nrm(4, (DEPTH, DEC_BATCH, CONV_WIDTH - 1, D_RNN), 1.0),
        'state_h': nrm(5, (DEPTH, DEC_BATCH, D_RNN), 0.5),
        'page_table': page_table,
        'c_prompt': nrm(8, (BATCH, D_MODEL), 1.0),
        'c_sample': nrm(9, (DEC_BATCH, D_MODEL), 1.0),
        'w_ada': nrm(10, (DEPTH, D_MODEL, N_MOD * D_MODEL), 0.3 * sd),
        'b_ada': nrm(11, (DEPTH, N_MOD * D_MODEL), 0.02),
        'norm_mix': 1.0 + nrm(12, (DEPTH, D_MODEL), 0.02),
        'norm_ff': 1.0 + nrm(13, (DEPTH, D_MODEL), 0.02),
        'w_in': nrm(14, (DEPTH, D_MODEL, D_IN), sd),
        'q_gain': 1.0 + nrm(15, (DEPTH, HEAD_DIM), 0.02),
        'k_gain': 1.0 + nrm(16, (DEPTH, HEAD_DIM), 0.02),
        'conv_w': nrm(17, (DEPTH, CONV_WIDTH, D_RNN), 0.5),
        'conv_b': nrm(18, (DEPTH, D_RNN), 0.02),
        'lru_wa': nrm(19, (DEPTH, N_RNN_BLOCKS, RNN_BLOCK, RNN_BLOCK), RNN_BLOCK ** -0.5),
        'lru_ba': nrm(20, (DEPTH, N_RNN_BLOCKS, RNN_BLOCK), 0.02),
        'lru_wx': nrm(21, (DEPTH, N_RNN_BLOCKS, RNN_BLOCK, RNN_BLOCK), RNN_BLOCK ** -0.5),
        'lru_bx': nrm(22, (DEPTH, N_RNN_BLOCKS, RNN_BLOCK), 0.02),
        'lru_lambda': lru_lambda,
        'attn_out_gain': 1.0 + nrm(23, (DEPTH, D_ATTN), 0.02),
        'rnn_out_gain': 1.0 + nrm(24, (DEPTH, D_RNN), 0.02),
        'w_out': nrm(25, (DEPTH, D_MODEL, D_MODEL), sd),
        'peer_wq': nrm(26, (DEPTH, D_MODEL, PEER_HEADS * PEER_DKEY), sd),
        'peer_keys': nrm(27, (DEPTH, PEER_HEADS, 2, PEER_NKEYS, PEER_DKEY // 2), (PEER_DKEY // 2) ** -0.5),
        'peer_u': nrm(28, (DEPTH, N_EXPERTS, D_MODEL), sd),
        'peer_v': nrm(29, (DEPTH, N_EXPERTS, D_MODEL), 0.5),
    }


def reference(x_prompt, x_sample, cache_k, cache_v, state_conv, state_h, page_table, c_prompt, c_sample,
              w_ada, b_ada, norm_mix, norm_ff, w_in, q_gain, k_gain, conv_w, conv_b,
              lru_wa, lru_ba, lru_wx, lru_bx, lru_lambda, attn_out_gain, rnn_out_gain, w_out,
              peer_wq, peer_keys, peer_u, peer_v):
    yp, ys = x_prompt, x_sample
    B = x_prompt.shape[0]
    page_sum_k = jnp.sum(cache_k, axis=2, dtype=jnp.float32)
    kp_l, vp_l, cp_l, hp_l, ks_l, vs_l, cs_l, hs_l = [], [], [], [], [], [], [], []
    for l in range(DEPTH):
        lw = (w_ada[l], b_ada[l], norm_mix[l], norm_ff[l], w_in[l], q_gain[l], k_gain[l],
              conv_w[l], conv_b[l], lru_wa[l], lru_ba[l], lru_wx[l], lru_bx[l], lru_lambda[l],
              attn_out_gain[l], rnn_out_gain[l], w_out[l], peer_wq[l], peer_keys[l], peer_u[l], peer_v[l])
        yp, kp, vp, cp, hp = decoder_layer(
            yp, c_prompt, moba_prompt,
            jnp.zeros((B, CONV_WIDTH - 1, D_RNN), yp.dtype), jnp.zeros((B, D_RNN), yp.dtype), *lw)
        attend_s = lambda q, k, v: moba_sample(q, k, v, cache_k, cache_v, l, page_sum_k[l], page_table)
        ys, ks, vs, cs, hs = decoder_layer(ys, c_sample, attend_s, state_conv[l], state_h[l], *lw)
        kp_l.append(kp); vp_l.append(vp); cp_l.append(cp); hp_l.append(hp)
        ks_l.append(ks); vs_l.append(vs); cs_l.append(cs); hs_l.append(hs)
    return (yp, ys, jnp.stack(kp_l), jnp.stack(vp_l), jnp.stack(cp_l), jnp.stack(hp_l),
            jnp.stack(ks_l), jnp.stack(vs_l), jnp.stack(cs_l), jnp.stack(hs_l))
```

```python
import functools

import jax
import jax.numpy as jnp
from jax import lax
from jax.experimental import pallas as pl
from jax.experimental.pallas import tpu as pltpu

F32 = jnp.float32
BF16 = jnp.bfloat16

N_HEADS = 8
HEAD_DIM = 64
D_ATTN = N_HEADS * HEAD_DIM
CONV_WIDTH = 4
LRU_C = 8.0
MOBA_BLOCK = 256
MOBA_TOPK = 3
PEER_HEADS = 8
PEER_NKEYS = 128
PEER_TOPK = 16
N_MOD = 6
EPS = 1e-6

ROW_TILE = 512
PEER_EXPERT_TILE = 1024
SUBLANES = 8
NEG_BIG = -1e30
RANK_BIG = 1e6
VMEM_LIMIT = 48 * 1024 * 1024


def _cparams(n_axes, vmem=VMEM_LIMIT):
    return pltpu.CompilerParams(dimension_semantics=("arbitrary",) * n_axes, vmem_limit_bytes=vmem)


def _dot_nt(a, b, precision=None):
    return lax.dot_general(a, b, (((1,), (1,)), ((), ())), precision=precision, preferred_element_type=F32)


def _dot_tn(a, b):
    return lax.dot_general(a, b, (((0,), (0,)), ((), ())), preferred_element_type=F32)


def _gelu(x):
    return jax.nn.gelu(x)


def _ada_kernel(c_ref, w_ref, b_ref, o_ref):
    c = c_ref[...]
    s = c * jax.nn.sigmoid(c)
    o_ref[...] = jnp.dot(s.astype(BF16), w_ref[...].astype(BF16), preferred_element_type=F32) + b_ref[...]


def _ada(c_all, w_ada, b_ada):
    n, d = c_all.shape
    n_out = w_ada.shape[1]
    tile = d
    return pl.pallas_call(
        _ada_kernel,
        grid=(n_out // tile,),
        in_specs=[
            pl.BlockSpec((n, d), lambda j: (0, 0)),
            pl.BlockSpec((d, tile), lambda j: (0, j)),
            pl.BlockSpec((1, tile), lambda j: (0, j)),
        ],
        out_specs=pl.BlockSpec((n, tile), lambda j: (0, j)),
        out_shape=jax.ShapeDtypeStruct((n, n_out), F32),
        compiler_params=_cparams(1),
        name="ada",
    )(c_all, w_ada, b_ada.reshape(1, n_out))


def _proj_kernel(x_ref, mod_ref, nw_ref, win_ref, hm_ref, qg_ref, kg_ref, *out_refs, attn_extras):
    nb, t, d = x_ref.shape
    x = x_ref[...]
    mod = mod_ref[...]
    ms = jnp.mean(x * x, axis=-1, keepdims=True)
    h = x * lax.rsqrt(ms + EPS) * nw_ref[...]
    h = h * (1.0 + mod[:, 1:2, :]) + mod[:, 0:1, :]
    proj = jnp.dot(h.reshape(nb * t, d).astype(BF16), win_ref[...], preferred_element_type=F32)
    q = proj[:, 0:D_ATTN]
    k = proj[:, D_ATTN:2 * D_ATTN]
    v = proj[:, 2 * D_ATTN:3 * D_ATTN]
    d_rnn = (proj.shape[1] - 3 * D_ATTN) // 2
    xr = proj[:, 3 * D_ATTN:3 * D_ATTN + d_rnn]
    gr = proj[:, 3 * D_ATTN + d_rnn:]
    hm = hm_ref[...]

    def head_norm(a, gain):
        msq = jnp.dot((a * a).astype(BF16), hm, preferred_element_type=F32)
        return a * lax.rsqrt(msq + EPS) * gain

    q = head_norm(q, qg_ref[...])
    k = head_norm(k, kg_ref[...])
    q_ref, k_ref, v_ref, xr_ref, gr_ref = out_refs[:5]
    q_ref[...] = q
    k_ref[...] = k
    v_ref[...] = v
    xr_ref[...] = xr
    gr_ref[...] = gr
    if attn_extras:
        kbf_ref, vbf_ref, ksum_ref = out_refs[5:]
        kbf_ref[...] = k.astype(BF16)
        vbf_ref[...] = v.astype(BF16)
        n_blk = (nb * t) // MOBA_BLOCK
        sums = [jnp.sum(k[i * MOBA_BLOCK:(i + 1) * MOBA_BLOCK], axis=0, keepdims=True) for i in range(n_blk)]
        ksum_ref[0] = jnp.concatenate(sums, axis=0)


def _proj(x, mod, norm_w, w_in_bf, head_mean, q_gain_t, k_gain_t, nb, t, attn_extras):
    bt, tt, d = x.shape
    g0, g1 = bt // nb, tt // t
    rows = nb * t
    n_rows = bt * tt
    d_in = w_in_bf.shape[1]
    d_rnn = (d_in - 3 * D_ATTN) // 2
    row_spec = lambda w: pl.BlockSpec((rows, w), lambda i, j: (i * g1 + j, 0))
    const = lambda shape: pl.BlockSpec(shape, lambda i, j: (0,) * len(shape))
    out_specs = [row_spec(D_ATTN)] * 3 + [row_spec(d_rnn)] * 2
    out_shape = [jax.ShapeDtypeStruct((n_rows, D_ATTN), F32)] * 3 + [jax.ShapeDtypeStruct((n_rows, d_rnn), F32)] * 2
    if attn_extras:
        n_blk = rows // MOBA_BLOCK
        out_specs += [row_spec(D_ATTN)] * 2 + [pl.BlockSpec((1, n_blk, D_ATTN), lambda i, j: (i * g1 + j, 0, 0))]
        out_shape += [jax.ShapeDtypeStruct((n_rows, D_ATTN), BF16)] * 2
        out_shape += [jax.ShapeDtypeStruct((g0 * g1, n_blk, D_ATTN), F32)]
    return pl.pallas_call(
        functools.partial(_proj_kernel, attn_extras=attn_extras),
        grid=(g0, g1),
        in_specs=[
            pl.BlockSpec((nb, t, d), lambda i, j: (i, j, 0)),
            pl.BlockSpec((nb, N_MOD, d), lambda i, j: (i, 0, 0)),
            const((1, d)),
            const((d, d_in)),
            const((D_ATTN, D_ATTN)),
            const((1, D_ATTN)),
            const((1, D_ATTN)),
        ],
        out_specs=out_specs,
        out_shape=out_shape,
        compiler_params=_cparams(2),
        name="proj",
    )(x, mod, norm_w, w_in_bf, head_mean, q_gain_t, k_gain_t)


def _moba_prompt_kernel(q_ref, k_ref, v_ref, ksum_ref, o_ref):
    qb = pl.program_id(1)
    tq = q_ref.shape[0]
    s_len = k_ref.shape[0]
    n_blk = s_len // MOBA_BLOCK
    q = q_ref[...]
    kall = k_ref[...]
    vall = v_ref[...]
    kmean = ksum_ref[0] * (1.0 / MOBA_BLOCK)
    row = lax.broadcasted_iota(jnp.int32, (tq, s_len), 0)
    col = lax.broadcasted_iota(jnp.int32, (tq, s_len), 1)
    t_pos = qb * tq + row
    dist = (t_pos - col).astype(F32)
    own = ((col // MOBA_BLOCK) == qb) & (col <= t_pos)
    blk_of_col = lax.broadcasted_iota(jnp.int32, (n_blk, s_len), 1) // MOBA_BLOCK
    expand = (blk_of_col == lax.broadcasted_iota(jnp.int32, (n_blk, s_len), 0)).astype(BF16)
    n_iota = lax.broadcasted_iota(jnp.int32, (tq, n_blk), 1)
    outs = []
    for h in range(N_HEADS):
        hs = slice(h * HEAD_DIM, (h + 1) * HEAD_DIM)
        qh = q[:, hs]
        gate = _dot_nt(qh, kmean[:, hs], precision=lax.Precision.HIGHEST)
        cnt = jnp.zeros((tq, n_blk), jnp.int32)
        for m in range(n_blk):
            gm = gate[:, m:m + 1]
            beats = (gm > gate) | ((gm == gate) & (m < n_iota))
            cnt = cnt + jnp.where(beats, (m < qb).astype(jnp.int32), 0)
        sel = ((cnt < MOBA_TOPK) & (n_iota < qb)).astype(BF16)
        sel_cols = jnp.dot(sel, expand, preferred_element_type=F32)
        mask = own | (sel_cols > 0.5)
        slope = 2.0 ** (-8.0 * (h + 1) / N_HEADS)
        s = _dot_nt(qh.astype(BF16), kall[:, hs]) * (HEAD_DIM ** -0.5) - slope * dist
        s = jnp.where(mask, s, NEG_BIG)
        mx = jnp.max(s, axis=-1, keepdims=True)
        p = jnp.exp(s - mx)
        l = jnp.sum(p, axis=-1, keepdims=True)
        o = jnp.dot(p.astype(BF16), vall[:, hs], preferred_element_type=F32)
        outs.append(o / l)
    o_ref[...] = jnp.concatenate(outs, axis=1)


def _moba_prompt(q, kbf, vbf, ksum, batch, seq):
    nq = seq // MOBA_BLOCK
    return pl.pallas_call(
        _moba_prompt_kernel,
        grid=(batch, nq),
        in_specs=[
            pl.BlockSpec((MOBA_BLOCK, D_ATTN), lambda b, i: (b * nq + i, 0)),
            pl.BlockSpec((seq, D_ATTN), lambda b, i: (b, 0)),
            pl.BlockSpec((seq, D_ATTN), lambda b, i: (b, 0)),
            pl.BlockSpec((1, nq, D_ATTN), lambda b, i: (b, 0, 0)),
        ],
        out_specs=pl.BlockSpec((MOBA_BLOCK, D_ATTN), lambda b, i: (b * nq + i, 0)),
        out_shape=jax.ShapeDtypeStruct((batch * seq, D_ATTN), F32),
        compiler_params=_cparams(2),
        name="moba_prompt",
    )(q, kbf, vbf, ksum)


def _moba_sample_kernel(pt_ref, q_ref, kn_ref, vn_ref, k0_ref, k1_ref, v0_ref, v1_ref, o_ref,
                        qbd_scr, gate_scr, m_scr, l_scr, o_scr, *, past_len):
    del pt_ref
    n = pl.program_id(1)
    n_full = pl.num_programs(1)
    t_new = q_ref.shape[1]
    rows = N_HEADS * t_new
    r_iota = lax.broadcasted_iota(jnp.int32, (rows, 1), 0)
    t_of_r = r_iota % t_new
    h_of_r = r_iota // t_new
    slope = jnp.exp2(-8.0 * (h_of_r + 1).astype(F32) / N_HEADS)
    lane = lax.broadcasted_iota(jnp.int32, (rows, 128), 1)

    @pl.when(n == 0)
    def _():
        qt = jnp.concatenate([q_ref[0]] * N_HEADS, axis=0)
        rr = lax.broadcasted_iota(jnp.int32, (rows, D_ATTN), 0) // t_new
        cc = lax.broadcasted_iota(jnp.int32, (rows, D_ATTN), 1) // HEAD_DIM
        qbd_scr[...] = jnp.where(rr == cc, qt, 0.0)
        gate_scr[...] = jnp.full((rows, 128), -jnp.inf, F32)
        m_scr[...] = jnp.zeros((rows, 128), F32)
        l_scr[...] = jnp.zeros((rows, 128), F32)

    qbd = qbd_scr[...]
    kblk = jnp.concatenate([k0_ref[0], k1_ref[0]], axis=0)
    vblk = jnp.concatenate([v0_ref[0], v1_ref[0]], axis=0)
    kmean = jnp.sum(kblk, axis=0, keepdims=True) * (1.0 / MOBA_BLOCK)
    gate_n = jnp.sum(qbd * kmean, axis=1, keepdims=True)
    s = _dot_nt(qbd.astype(BF16), kblk.astype(BF16)) * (HEAD_DIM ** -0.5)
    pos = n * MOBA_BLOCK + lax.broadcasted_iota(jnp.int32, (rows, MOBA_BLOCK), 1)
    dist = ((past_len + t_of_r) - pos).astype(F32)
    s = s - slope * dist
    m_n = jnp.max(s, axis=1, keepdims=True)
    p = jnp.exp(s - m_n)
    l_n = jnp.sum(p, axis=1, keepdims=True)
    o_scr[n] = jnp.dot(p.astype(BF16), vblk.astype(BF16), preferred_element_type=F32)
    here = lane == n
    gate_scr[...] = jnp.where(here, gate_n, gate_scr[...])
    m_scr[...] = jnp.where(here, m_n, m_scr[...])
    l_scr[...] = jnp.where(here, l_n, l_scr[...])

    @pl.when(n == n_full - 1)
    def _():
        work = gate_scr[...]
        sel = jnp.zeros((rows, 128), jnp.bool_)
        for _k in range(MOBA_TOPK):
            mx = jnp.max(work, axis=1, keepdims=True)
            idx = jnp.min(jnp.where(work == mx, lane, 128), axis=1, keepdims=True)
            hit = lane == idx
            sel = sel | hit
            work = jnp.where(hit, -jnp.inf, work)
        kn = kn_ref[0]
        vn = vn_ref[0]
        sn = _dot_nt(qbd.astype(BF16), kn.astype(BF16)) * (HEAD_DIM ** -0.5)
        dn = t_of_r - lax.broadcasted_iota(jnp.int32, (rows, t_new), 1)
        sn = jnp.where(dn >= 0, sn - slope * dn.astype(F32), NEG_BIG)
        m_all = m_scr[...]
        big_m = jnp.maximum(jnp.max(jnp.where(sel, m_all, NEG_BIG), axis=1, keepdims=True),
                            jnp.max(sn, axis=1, keepdims=True))
        w = jnp.where(sel, jnp.exp(jnp.where(sel, m_all, big_m) - big_m), 0.0)
        pn = jnp.exp(sn - big_m)
        l_tot = jnp.sum(w * l_scr[...], axis=1, keepdims=True) + jnp.sum(pn, axis=1, keepdims=True)
        acc = jnp.dot(pn.astype(BF16), vn.astype(BF16), preferred_element_type=F32)
        for blk in range(o_scr.shape[0]):
            acc = acc + w[:, blk:blk + 1] * o_scr[blk]
        acc = acc / l_tot
        head_of_lane = lax.broadcasted_iota(jnp.int32, (t_new, D_ATTN), 1) // HEAD_DIM
        out = jnp.zeros((t_new, D_ATTN), F32)
        for h in range(N_HEADS):
            out = out + jnp.where(head_of_lane == h, acc[h * t_new:(h + 1) * t_new], 0.0)
        o_ref[0] = out


def _moba_sample(q, k_new, v_new, cache_k_pages, cache_v_pages, page_table_flat, page_base, n_pages, page_size):
    dec_batch, t_new, _ = q.shape
    pages_per_block = MOBA_BLOCK // page_size
    assert pages_per_block == 2
    n_full = n_pages // pages_per_block
    assert n_full * pages_per_block == n_pages and MOBA_TOPK <= n_full <= 128
    rows = N_HEADS * t_new
    tok_spec = pl.BlockSpec((1, t_new, D_ATTN), lambda b, n, pt: (b, 0, 0))

    def page_spec(which):
        return pl.BlockSpec((1, page_size, D_ATTN),
                            lambda b, n, pt: (page_base + pt[b * n_pages + pages_per_block * n + which], 0, 0))

    grid_spec = pltpu.PrefetchScalarGridSpec(
        num_scalar_prefetch=1,
        grid=(dec_batch, n_full),
        in_specs=[tok_spec, tok_spec, tok_spec, page_spec(0), page_spec(1), page_spec(0), page_spec(1)],
        out_specs=pl.BlockSpec((1, t_new, D_ATTN), lambda b, n, pt: (b, 0, 0)),
        scratch_shapes=[
            pltpu.VMEM((rows, D_ATTN), F32),
            pltpu.VMEM((rows, 128), F32),
            pltpu.VMEM((rows, 128), F32),
            pltpu.VMEM((rows, 128), F32),
            pltpu.VMEM((n_full, rows, D_ATTN), F32),
        ],
    )
    return pl.pallas_call(
        functools.partial(_moba_sample_kernel, past_len=n_pages * page_size),
        grid_spec=grid_spec,
        out_shape=jax.ShapeDtypeStruct((dec_batch, t_new, D_ATTN), F32),
        compiler_params=_cparams(2),
        name="moba_sample",
    )(page_table_flat, q, k_new, v_new, cache_k_pages, cache_k_pages, cache_v_pages, cache_v_pages)


def _softplus(x):
    return jnp.maximum(x, 0.0) + jnp.log1p(jnp.exp(-jnp.abs(x)))


def _expm1(x):
    return jnp.tanh(0.5 * x) * (jnp.exp(x) + 1.0)


def _rglru_kernel(xr_ref, gr_ref, buf0_ref, h0_ref, cw_ref, cb_ref, wa_ref, ba_ref, wx_ref, bx_ref,
                  lam_ref, gain_ref, rnn_ref, conv_ref, hlast_ref, cx_scr, ch_scr):
    j = pl.program_id(1)
    nb, t, c = xr_ref.shape

    @pl.when(j == 0)
    def _():
        cx_scr[...] = buf0_ref[...]
        ch_scr[...] = h0_ref[...]

    x = xr_ref[...]
    ext = jnp.concatenate([cx_scr[...], x], axis=1)
    cw = cw_ref[...]
    xc = cb_ref[...]
    for tap in range(CONV_WIDTH - 1):
        shift = CONV_WIDTH - 1 - tap
        xc = xc + pltpu.roll(ext, shift, axis=1)[:, SUBLANES:, :] * cw[tap:tap + 1]
    xc = xc + x * cw[CONV_WIDTH - 1:CONV_WIDTH]

    x2 = xc.reshape(nb * t, c)
    xb = x2.astype(BF16)
    r = jax.nn.sigmoid(jnp.dot(xb, wa_ref[...], preferred_element_type=F32) + ba_ref[...])
    i = jax.nn.sigmoid(jnp.dot(xb, wx_ref[...], preferred_element_type=F32) + bx_ref[...])
    log_a = -LRU_C * r * _softplus(-lam_ref[...])
    a = jnp.exp(log_a)
    u = jnp.sqrt(-_expm1(2.0 * log_a)) * (i * x2)

    a3 = a.reshape(nb, t, c)
    b3 = u.reshape(nb, t, c)
    row = lax.broadcasted_iota(jnp.int32, (nb, t, c), 1)
    step = 1
    while step < t:
        a_prev = jnp.where(row >= step, pltpu.roll(a3, step, axis=1), 1.0)
        b_prev = jnp.where(row >= step, pltpu.roll(b3, step, axis=1), 0.0)
        b3 = a3 * b_prev + b3
        a3 = a3 * a_prev
        step *= 2
    h = a3 * ch_scr[...] + b3

    ch_scr[...] = h[:, t - 1:t, :]
    tail = x[:, t - SUBLANES:, :]
    cx_scr[...] = tail
    conv_ref[...] = tail
    hlast_ref[...] = h[:, t - SUBLANES:, :]

    rnn = h * _gelu(gr_ref[...])
    ms = jnp.mean(rnn * rnn, axis=-1, keepdims=True)
    rnn_ref[...] = (rnn * lax.rsqrt(ms + EPS) * gain_ref[...]).reshape(nb * t, c)


def _rglru(xr, gr, buf0, h0, conv_w, conv_b, wa_bd, ba, wx_bd, bx, lam, gain, nb, t):
    bt, tt, c = xr.shape
    g0, g1 = bt // nb, tt // t
    const = lambda shape: pl.BlockSpec(shape, lambda i, j: (0,) * len(shape))
    per_b = lambda r: pl.BlockSpec((nb, r, c), lambda i, j: (i, 0, 0))
    return pl.pallas_call(
        _rglru_kernel,
        grid=(g0, g1),
        in_specs=[
            pl.BlockSpec((nb, t, c), lambda i, j: (i, j, 0)),
            pl.BlockSpec((nb, t, c), lambda i, j: (i, j, 0)),
            per_b(SUBLANES),
            per_b(1),
            const((CONV_WIDTH, c)),
            const((1, c)),
            const((c, c)),
            const((1, c)),
            const((c, c)),
            const((1, c)),
            const((1, c)),
            const((1, c)),
        ],
        out_specs=[
            pl.BlockSpec((nb * t, c), lambda i, j: (i * g1 + j, 0)),
            per_b(SUBLANES),
            per_b(SUBLANES),
        ],
        out_shape=[
            jax.ShapeDtypeStruct((bt * tt, c), F32),
            jax.ShapeDtypeStruct((bt, SUBLANES, c), F32),
            jax.ShapeDtypeStruct((bt, SUBLANES, c), F32),
        ],
        scratch_shapes=[pltpu.VMEM((nb, SUBLANES, c), F32), pltpu.VMEM((nb, 1, c), F32)],
        compiler_params=_cparams(2),
        name="rglru",
    )(xr, gr, buf0, h0, conv_w, conv_b, wa_bd, ba, wx_bd, bx, lam, gain)


def _mix_kernel(attn_ref, rnn_ref, x_ref, mod_ref, ag_ref, wout_ref, nff_ref, wq_ref, x1_ref, h2_ref, qp_ref):
    nb, t, d = x_ref.shape
    attn = attn_ref[...]
    ms = jnp.mean(attn * attn, axis=-1, keepdims=True)
    attn_n = attn * lax.rsqrt(ms + EPS) * ag_ref[...]
    mixed = jnp.dot(attn_n.astype(BF16), wout_ref[0:D_ATTN, :], preferred_element_type=F32)
    mixed = mixed + jnp.dot(rnn_ref[...].astype(BF16), wout_ref[D_ATTN:, :], preferred_element_type=F32)
    mod = mod_ref[...]
    x1 = x_ref[...] + mod[:, 2:3, :] * mixed.reshape(nb, t, d)
    ms1 = jnp.mean(x1 * x1, axis=-1, keepdims=True)
    h = x1 * lax.rsqrt(ms1 + EPS) * nff_ref[...]
    h = h * (1.0 + mod[:, 4:5, :]) + mod[:, 3:4, :]
    h2 = h.reshape(nb * t, d).astype(BF16)
    x1_ref[...] = x1
    h2_ref[...] = h2
    qp_ref[...] = jnp.dot(h2, wq_ref[...], preferred_element_type=F32).astype(BF16)


def _mix(attn, rnn_n, x, mod, attn_gain, w_out_bf, norm_ff, wq_bf, nb, t):
    bt, tt, d = x.shape
    g0, g1 = bt // nb, tt // t
    rows = nb * t
    n_rows = bt * tt
    d_rnn = rnn_n.shape[1]
    dq = wq_bf.shape[1]
    const = lambda shape: pl.BlockSpec(shape, lambda i, j: (0,) * len(shape))
    row_spec = lambda w: pl.BlockSpec((rows, w), lambda i, j: (i * g1 + j, 0))
    return pl.pallas_call(
        _mix_kernel,
        grid=(g0, g1),
        in_specs=[
            row_spec(D_ATTN),
            row_spec(d_rnn),
            pl.BlockSpec((nb, t, d), lambda i, j: (i, j, 0)),
            pl.BlockSpec((nb, N_MOD, d), lambda i, j: (i, 0, 0)),
            const((1, D_ATTN)),
            const((D_ATTN + d_rnn, d)),
            const((1, d)),
            const((d, dq)),
        ],
        out_specs=[pl.BlockSpec((nb, t, d), lambda i, j: (i, j, 0)), row_spec(d), row_spec(dq)],
        out_shape=[
            jax.ShapeDtypeStruct((bt, tt, d), F32),
            jax.ShapeDtypeStruct((n_rows, d), BF16),
            jax.ShapeDtypeStruct((n_rows, dq), BF16),
        ],
        compiler_params=_cparams(2),
        name="mix",
    )(attn, rnn_n, x, mod, attn_gain, w_out_bf, norm_ff, wq_bf)


def _extract_topk(work, k):
    n_rows = work.shape[0]
    rid = lax.broadcasted_iota(jnp.int32, work.shape, 0).astype(F32)
    rank = jnp.full(work.shape, RANK_BIG, F32)
    vals = []
    for step in range(k):
        mx = jnp.max(work, axis=0, keepdims=True)
        idx = jnp.min(jnp.where(work == mx, rid, float(n_rows)), axis=0, keepdims=True)
        hit = rid == idx
        rank = jnp.where(hit, float(step), rank)
        work = jnp.where(hit, -jnp.inf, work)
        vals.append(mx)
    return jnp.concatenate(vals, axis=0), rank


def _peer_kernel(x1_ref, mod_ref, h2_ref, qp_ref, keys_ref, u_ref, v_ref, y_ref,
                 s0m_scr, e0z_scr, r0_scr, s1m_scr, e1_scr, r1_scr, tau_scr, ptau_scr, wt_scr, acc_scr, flag_ref):
    et = pl.program_id(2)
    n_et = pl.num_programs(2)
    nb, t, d = x1_ref.shape
    c_tok = nb * t
    te = u_ref.shape[0]
    i_per_tile = te // PEER_NKEYS

    @pl.when(et == 0)
    def _select():
        bad = jnp.zeros((1, c_tok), F32)
        for h in range(PEER_HEADS):
            halves = []
            for p in range(2):
                hp = 2 * h + p
                s_t = _dot_nt(keys_ref[hp], qp_ref[:, hp * PEER_NKEYS:(hp + 1) * PEER_NKEYS])
                sv, rank = _extract_topk(s_t, PEER_TOPK)
                member = rank < PEER_TOPK
                e = jnp.where(member, jnp.exp(s_t - sv[0:1]), 0.0)
                halves.append((sv, rank, jnp.where(member, s_t, -jnp.inf), e))
            sv0, rank0, s0m, e0 = halves[0]
            sv1, rank1, s1m, e1 = halves[1]
            cand = jnp.concatenate([sv0[k0:k0 + 1] + sv1 for k0 in range(PEER_TOPK)], axis=0)
            fv, crank = _extract_topk(cand, PEER_TOPK)
            tau = fv[PEER_TOPK - 1:PEER_TOPK]
            z = jnp.sum(jnp.exp(fv - fv[0:1]), axis=0, keepdims=True)
            n_ge = jnp.sum(jnp.where(cand >= tau, 1.0, 0.0), axis=0, keepdims=True)
            bad = jnp.maximum(bad, jnp.where(n_ge != float(PEER_TOPK), 1.0, 0.0))
            pos = lax.broadcasted_iota(jnp.int32, cand.shape, 0).astype(F32)
            ptau = jnp.sum(jnp.where(crank == float(PEER_TOPK - 1), pos, 0.0), axis=0, keepdims=True)
            s0m_scr[h] = s0m
            e0z_scr[h] = e0 / z
            r0_scr[h] = rank0 * float(PEER_TOPK)
            s1m_scr[h] = s1m
            e1_scr[h] = e1
            r1_scr[h] = rank1
            tau_scr[h] = tau
            ptau_scr[h] = ptau
        flag_ref[0] = jnp.max(bad).astype(jnp.int32)
        acc_scr[...] = jnp.zeros(acc_scr.shape, F32)

    a_t = _dot_nt(u_ref[...], h2_ref[...])
    act = _gelu(a_t)

    def weights(exact_ties):
        for ii in range(i_per_tile):
            i = et * i_per_tile + ii
            g = jnp.zeros((PEER_NKEYS, c_tok), F32)
            for h in range(PEER_HEADS):
                tsum = s1m_scr[h] + s0m_scr[h, pl.ds(i, 1), :]
                val = e1_scr[h] * e0z_scr[h, pl.ds(i, 1), :]
                tau = tau_scr[h]
                if exact_ties:
                    pos = r1_scr[h] + r0_scr[h, pl.ds(i, 1), :]
                    keep = (tsum > tau) | ((tsum == tau) & (pos <= ptau_scr[h]))
                else:
                    keep = tsum >= tau
                g = g + jnp.where(keep, val, 0.0)
            rows = slice(ii * PEER_NKEYS, (ii + 1) * PEER_NKEYS)
            wt_scr[rows, :] = (g * act[rows]).astype(BF16)

    @pl.when(flag_ref[0] == 0)
    def _():
        weights(False)

    @pl.when(flag_ref[0] != 0)
    def _():
        weights(True)

    acc_scr[...] += _dot_tn(wt_scr[...], v_ref[...])

    @pl.when(et == n_et - 1)
    def _():
        y_ref[...] = x1_ref[...] + mod_ref[...][:, 5:6, :] * acc_scr[...].reshape(nb, t, d)


def _peer(x1, mod, h2, qp, keys_bf, u_bf, v_bf, nb, t):
    bt, tt, d = x1.shape
    g0, g1 = bt // nb, tt // t
    c_tok = nb * t
    n_exp = u_bf.shape[0]
    te = PEER_EXPERT_TILE
    n_et = n_exp // te
    dq = qp.shape[1]
    tok3 = pl.BlockSpec((nb, t, d), lambda i, j, e: (i, j, 0))
    per_head = lambda r: pltpu.VMEM((PEER_HEADS, r, c_tok), F32)
    return pl.pallas_call(
        _peer_kernel,
        grid=(g0, g1, n_et),
        in_specs=[
            tok3,
            pl.BlockSpec((nb, N_MOD, d), lambda i, j, e: (i, 0, 0)),
            pl.BlockSpec((c_tok, d), lambda i, j, e: (i * g1 + j, 0)),
            pl.BlockSpec((c_tok, dq), lambda i, j, e: (i * g1 + j, 0)),
            pl.BlockSpec(keys_bf.shape, lambda i, j, e: (0, 0, 0)),
            pl.BlockSpec((te, d), lambda i, j, e: (e, 0)),
            pl.BlockSpec((te, d), lambda i, j, e: (e, 0)),
        ],
        out_specs=tok3,
        out_shape=jax.ShapeDtypeStruct((bt, tt, d), F32),
        scratch_shapes=[
            per_head(PEER_NKEYS), per_head(PEER_NKEYS), per_head(PEER_NKEYS),
            per_head(PEER_NKEYS), per_head(PEER_NKEYS), per_head(PEER_NKEYS),
            per_head(1), per_head(1),
            pltpu.VMEM((te, c_tok), BF16),
            pltpu.VMEM((c_tok, d), F32),
            pltpu.SMEM((1,), jnp.int32),
        ],
        compiler_params=_cparams(3),
        name="peer",
    )(x1, mod, h2, qp, keys_bf, u_bf, v_bf)


def _block_diag(w):
    n, c, d = w.shape
    eye = jnp.eye(n, dtype=w.dtype)
    return (eye[:, None, :, None] * w[:, :, None, :]).reshape(n * c, n * d)


def kernel(x_prompt, x_sample, cache_k, cache_v, state_conv, state_h, page_table, c_prompt, c_sample, w_ada, b_ada, norm_mix, norm_ff, w_in, q_gain, k_gain, conv_w, conv_b, lru_wa, lru_ba, lru_wx, lru_bx, lru_lambda, attn_out_gain, rnn_out_gain, w_out, peer_wq, peer_keys, peer_u, peer_v):
    depth = w_ada.shape[0]
    batch, seq, d_model = x_prompt.shape
    dec_batch, dec_seq, _ = x_sample.shape
    _, n_pool, page_size, n_heads, head_dim = cache_k.shape
    n_pages = page_table.shape[1]
    d_rnn = d_model - D_ATTN
    assert (n_heads, head_dim) == (N_HEADS, HEAD_DIM)
    assert seq % ROW_TILE == 0 and ROW_TILE % MOBA_BLOCK == 0 and ROW_TILE % dec_seq == 0
    assert dec_seq == SUBLANES and (dec_batch * dec_seq) % ROW_TILE == 0
    assert (n_pages * page_size) % MOBA_BLOCK == 0

    head_mean = _block_diag(jnp.full((N_HEADS, HEAD_DIM, HEAD_DIM), 1.0 / HEAD_DIM, F32)).astype(BF16)
    cache_k_pages = cache_k.reshape(depth * n_pool, page_size, D_ATTN)
    cache_v_pages = cache_v.reshape(depth * n_pool, page_size, D_ATTN)
    page_table_flat = page_table.reshape(-1).astype(jnp.int32)
    nb_s = ROW_TILE // dec_seq

    yp, ys = x_prompt, x_sample
    outs = [[] for _ in range(8)]
    for l in range(depth):
        mod = _ada(jnp.concatenate([c_prompt, c_sample], axis=0), w_ada[l], b_ada[l])
        mod = mod.reshape(batch + dec_batch, N_MOD, d_model)
        mod_p, mod_s = mod[:batch], mod[batch:]
        w_in_bf = w_in[l].astype(BF16)
        w_out_bf = w_out[l].astype(BF16)
        wq_bf = peer_wq[l].astype(BF16)
        keys_bf = peer_keys[l].reshape(2 * PEER_HEADS, PEER_NKEYS, -1).astype(BF16)
        u_bf = peer_u[l].astype(BF16)
        v_bf = peer_v[l].astype(BF16)
        wa_bd = _block_diag(lru_wa[l]).astype(BF16)
        wx_bd = _block_diag(lru_wx[l]).astype(BF16)
        row = lambda a: a.reshape(1, -1)
        q_gain_t = row(jnp.tile(q_gain[l], N_HEADS))
        k_gain_t = row(jnp.tile(k_gain[l], N_HEADS))
        rglru_w = (conv_w[l], row(conv_b[l]), wa_bd, row(lru_ba[l]), wx_bd, row(lru_bx[l]),
                   row(lru_lambda[l]), row(rnn_out_gain[l]))

        q, k, v, xr, gr, kbf, vbf, ksum = _proj(yp, mod_p, row(norm_mix[l]), w_in_bf, head_mean, q_gain_t, k_gain_t,
                                                1, ROW_TILE, True)
        attn = _moba_prompt(q, kbf, vbf, ksum.reshape(batch, seq // MOBA_BLOCK, D_ATTN), batch, seq)
        rnn_n, conv_p, h_p = _rglru(xr.reshape(batch, seq, d_rnn), gr.reshape(batch, seq, d_rnn),
                                    jnp.zeros((batch, SUBLANES, d_rnn), F32), jnp.zeros((batch, 1, d_rnn), F32),
                                    *rglru_w, 1, ROW_TILE)
        x1, h2, qp = _mix(attn, rnn_n, yp, mod_p, row(attn_out_gain[l]), w_out_bf, row(norm_ff[l]), wq_bf, 1, ROW_TILE)
        yp = _peer(x1, mod_p, h2, qp, keys_bf, u_bf, v_bf, 1, ROW_TILE)
        outs[0].append(k.reshape(batch, seq, N_HEADS, HEAD_DIM))
        outs[1].append(v.reshape(batch, seq, N_HEADS, HEAD_DIM))
        outs[2].append(conv_p[:, SUBLANES - (CONV_WIDTH - 1):])
        outs[3].append(h_p[:, SUBLANES - 1])

        q, k, v, xr, gr = _proj(ys, mod_s, row(norm_mix[l]), w_in_bf, head_mean, q_gain_t, k_gain_t,
                                nb_s, dec_seq, False)
        tok3 = lambda a: a.reshape(dec_batch, dec_seq, -1)
        attn = _moba_sample(tok3(q), tok3(k), tok3(v), cache_k_pages, cache_v_pages, page_table_flat,
                            l * n_pool, n_pages, page_size)
        buf0 = jnp.pad(state_conv[l], ((0, 0), (SUBLANES - (CONV_WIDTH - 1), 0), (0, 0)))
        rnn_n, conv_s, h_s = _rglru(tok3(xr), tok3(gr), buf0, state_h[l][:, None, :], *rglru_w, nb_s, dec_seq)
        x1, h2, qp = _mix(attn.reshape(dec_batch * dec_seq, D_ATTN), rnn_n, ys, mod_s, row(attn_out_gain[l]),
                          w_out_bf, row(norm_ff[l]), wq_bf, nb_s, dec_seq)
        ys = _peer(x1, mod_s, h2, qp, keys_bf, u_bf, v_bf, nb_s, dec_seq)
        outs[4].append(k.reshape(dec_batch, dec_seq, N_HEADS, HEAD_DIM))
        outs[5].append(v.reshape(dec_batch, dec_seq, N_HEADS, HEAD_DIM))
        outs[6].append(conv_s[:, SUBLANES - (CONV_WIDTH - 1):])
        outs[7].append(h_s[:, SUBLANES - 1])
    return (yp, ys) + tuple(jnp.stack(o) for o in outs)
```

```python
import functools

import jax
import jax.numpy as jnp
from jax import lax
from jax.experimental import pallas as pl
from jax.experimental.pallas import tpu as pltpu

F32 = jnp.float32
BF16 = jnp.bfloat16

N_HEADS = 8
HEAD_DIM = 64
D_ATTN = N_HEADS * HEAD_DIM
CONV_WIDTH = 4
LRU_C = 8.0
MOBA_BLOCK = 256
MOBA_TOPK = 3
PEER_HEADS = 8
PEER_NKEYS = 128
PEER_TOPK = 16
N_MOD = 6
EPS = 1e-6

ROW_TILE = 512
PEER_EXPERT_TILE = 1024
MOBA_SAMPLE_BLOCKS_PER_STEP = 4
SUBLANES = 8
LANES = 128
NEG_BIG = -1e30
RANK_BIG = 1e6
VMEM_LIMIT = 48 * 1024 * 1024


def _cparams(n_axes, vmem=VMEM_LIMIT):
    return pltpu.CompilerParams(dimension_semantics=("arbitrary",) * n_axes, vmem_limit_bytes=vmem)


def _dot_nt(a, b, precision=None):
    return lax.dot_general(a, b, (((1,), (1,)), ((), ())), precision=precision, preferred_element_type=F32)


def _gelu(x):
    return jax.nn.gelu(x)


def _ada_kernel(c_ref, w_ref, b_ref, o_ref):
    c = c_ref[...]
    s = c * jax.nn.sigmoid(c)
    o_ref[...] = jnp.dot(s.astype(BF16), w_ref[...].astype(BF16), preferred_element_type=F32) + b_ref[...]


def _ada(c_all, w_ada, b_ada):
    n, d = c_all.shape
    n_out = w_ada.shape[1]
    tile = d
    return pl.pallas_call(
        _ada_kernel,
        grid=(n_out // tile,),
        in_specs=[
            pl.BlockSpec((n, d), lambda j: (0, 0)),
            pl.BlockSpec((d, tile), lambda j: (0, j)),
            pl.BlockSpec((1, tile), lambda j: (0, j)),
        ],
        out_specs=pl.BlockSpec((n, tile), lambda j: (0, j)),
        out_shape=jax.ShapeDtypeStruct((n, n_out), F32),
        compiler_params=_cparams(1),
        name="ada",
    )(c_all, w_ada, b_ada.reshape(1, n_out))


def _proj_kernel(x_ref, mod_ref, nw_ref, win_ref, hm_ref, qg_ref, kg_ref, *out_refs, attn_extras):
    nb, t, d = x_ref.shape
    x = x_ref[...]
    mod = mod_ref[...]
    ms = jnp.mean(x * x, axis=-1, keepdims=True)
    h = x * lax.rsqrt(ms + EPS) * nw_ref[...]
    h = h * (1.0 + mod[:, 1:2, :]) + mod[:, 0:1, :]
    proj = jnp.dot(h.reshape(nb * t, d).astype(BF16), win_ref[...], preferred_element_type=F32)
    q = proj[:, 0:D_ATTN]
    k = proj[:, D_ATTN:2 * D_ATTN]
    v = proj[:, 2 * D_ATTN:3 * D_ATTN]
    d_rnn = (proj.shape[1] - 3 * D_ATTN) // 2
    xr = proj[:, 3 * D_ATTN:3 * D_ATTN + d_rnn]
    gr = proj[:, 3 * D_ATTN + d_rnn:]
    hm = hm_ref[...]

    def head_norm(a, gain):
        msq = jnp.dot((a * a).astype(BF16), hm, preferred_element_type=F32)
        return a * lax.rsqrt(msq + EPS) * gain

    q = head_norm(q, qg_ref[...])
    k = head_norm(k, kg_ref[...])
    q_ref, k_ref, v_ref, xr_ref, gr_ref = out_refs[:5]
    q_ref[...] = q
    k_ref[...] = k
    v_ref[...] = v
    xr_ref[...] = xr
    gr_ref[...] = gr
    if attn_extras:
        kbf_ref, vbf_ref, ksum_ref = out_refs[5:]
        kbf_ref[...] = k.astype(BF16)
        vbf_ref[...] = v.astype(BF16)
        n_blk = (nb * t) // MOBA_BLOCK
        sums = [jnp.sum(k[i * MOBA_BLOCK:(i + 1) * MOBA_BLOCK], axis=0, keepdims=True) for i in range(n_blk)]
        ksum_ref[0] = jnp.concatenate(sums, axis=0)


def _proj(x, mod, norm_w, w_in_bf, head_mean, q_gain_t, k_gain_t, nb, t, attn_extras):
    bt, tt, d = x.shape
    g0, g1 = bt // nb, tt // t
    rows = nb * t
    n_rows = bt * tt
    d_in = w_in_bf.shape[1]
    d_rnn = (d_in - 3 * D_ATTN) // 2
    row_spec = lambda w: pl.BlockSpec((rows, w), lambda i, j: (i * g1 + j, 0))
    const = lambda shape: pl.BlockSpec(shape, lambda i, j: (0,) * len(shape))
    out_specs = [row_spec(D_ATTN)] * 3 + [row_spec(d_rnn)] * 2
    out_shape = [jax.ShapeDtypeStruct((n_rows, D_ATTN), F32)] * 3 + [jax.ShapeDtypeStruct((n_rows, d_rnn), F32)] * 2
    if attn_extras:
        n_blk = rows // MOBA_BLOCK
        out_specs += [row_spec(D_ATTN)] * 2 + [pl.BlockSpec((1, n_blk, D_ATTN), lambda i, j: (i * g1 + j, 0, 0))]
        out_shape += [jax.ShapeDtypeStruct((n_rows, D_ATTN), BF16)] * 2
        out_shape += [jax.ShapeDtypeStruct((g0 * g1, n_blk, D_ATTN), F32)]
    return pl.pallas_call(
        functools.partial(_proj_kernel, attn_extras=attn_extras),
        grid=(g0, g1),
        in_specs=[
            pl.BlockSpec((nb, t, d), lambda i, j: (i, j, 0)),
            pl.BlockSpec((nb, N_MOD, d), lambda i, j: (i, 0, 0)),
            const((1, d)),
            const((d, d_in)),
            const((D_ATTN, D_ATTN)),
            const((1, D_ATTN)),
            const((1, D_ATTN)),
        ],
        out_specs=out_specs,
        out_shape=out_shape,
        compiler_params=_cparams(2),
        name="proj",
    )(x, mod, norm_w, w_in_bf, head_mean, q_gain_t, k_gain_t)


def _moba_prompt_kernel(q_ref, k_ref, v_ref, ksum_ref, o_ref):
    qb = pl.program_id(1)
    tq = q_ref.shape[0]
    s_len = k_ref.shape[0]
    n_blk = s_len // MOBA_BLOCK
    q = q_ref[...]
    kall = k_ref[...]
    vall = v_ref[...]
    kmean = ksum_ref[0] * (1.0 / MOBA_BLOCK)
    row = lax.broadcasted_iota(jnp.int32, (tq, s_len), 0)
    col = lax.broadcasted_iota(jnp.int32, (tq, s_len), 1)
    t_pos = qb * tq + row
    dist = (t_pos - col).astype(F32)
    own = ((col // MOBA_BLOCK) == qb) & (col <= t_pos)
    blk_of_col = lax.broadcasted_iota(jnp.int32, (n_blk, s_len), 1) // MOBA_BLOCK
    expand = (blk_of_col == lax.broadcasted_iota(jnp.int32, (n_blk, s_len), 0)).astype(BF16)
    n_iota = lax.broadcasted_iota(jnp.int32, (tq, n_blk), 1)
    outs = []
    for h in range(N_HEADS):
        hs = slice(h * HEAD_DIM, (h + 1) * HEAD_DIM)
        qh = q[:, hs]
        gate = _dot_nt(qh, kmean[:, hs], precision=lax.Precision.HIGHEST)
        cnt = jnp.zeros((tq, n_blk), jnp.int32)
        for m in range(n_blk):
            gm = gate[:, m:m + 1]
            beats = (gm > gate) | ((gm == gate) & (m < n_iota))
            cnt = cnt + jnp.where(beats, (m < qb).astype(jnp.int32), 0)
        sel = ((cnt < MOBA_TOPK) & (n_iota < qb)).astype(BF16)
        sel_cols = jnp.dot(sel, expand, preferred_element_type=F32)
        mask = own | (sel_cols > 0.5)
        slope = 2.0 ** (-8.0 * (h + 1) / N_HEADS)
        s = _dot_nt(qh.astype(BF16), kall[:, hs]) * (HEAD_DIM ** -0.5) - slope * dist
        s = jnp.where(mask, s, NEG_BIG)
        mx = jnp.max(s, axis=-1, keepdims=True)
        p = jnp.exp(s - mx)
        l = jnp.sum(p, axis=-1, keepdims=True)
        o = jnp.dot(p.astype(BF16), vall[:, hs], preferred_element_type=F32)
        outs.append(o / l)
    o_ref[...] = jnp.concatenate(outs, axis=1)


def _moba_prompt(q, kbf, vbf, ksum, batch, seq):
    nq = seq // MOBA_BLOCK
    return pl.pallas_call(
        _moba_prompt_kernel,
        grid=(batch, nq),
        in_specs=[
            pl.BlockSpec((MOBA_BLOCK, D_ATTN), lambda b, i: (b * nq + i, 0)),
            pl.BlockSpec((seq, D_ATTN), lambda b, i: (b, 0)),
            pl.BlockSpec((seq, D_ATTN), lambda b, i: (b, 0)),
            pl.BlockSpec((1, nq, D_ATTN), lambda b, i: (b, 0, 0)),
        ],
        out_specs=pl.BlockSpec((MOBA_BLOCK, D_ATTN), lambda b, i: (b * nq + i, 0)),
        out_shape=jax.ShapeDtypeStruct((batch * seq, D_ATTN), F32),
        compiler_params=_cparams(2),
        name="moba_prompt",
    )(q, kbf, vbf, ksum)


def _moba_sample_kernel(pt_ref, q_ref, kn_ref, vn_ref, *refs, past_len, blocks_per_step, pages_per_block):
    del pt_ref
    n_pg = blocks_per_step * pages_per_block
    k_refs, v_refs = refs[:n_pg], refs[n_pg:2 * n_pg]
    o_ref, qbd_scr, gate_scr, m_scr, l_scr, o_scr = refs[2 * n_pg:]
    step = pl.program_id(1)
    n_steps = pl.num_programs(1)
    t_new = q_ref.shape[1]
    rows = N_HEADS * t_new
    r_iota = lax.broadcasted_iota(jnp.int32, (rows, 1), 0)
    t_of_r = r_iota % t_new
    h_of_r = r_iota // t_new
    slope = jnp.exp2(-8.0 * (h_of_r + 1).astype(F32) / N_HEADS)
    lane = lax.broadcasted_iota(jnp.int32, (rows, LANES), 1)
    scale = HEAD_DIM ** -0.5

    @pl.when(step == 0)
    def _():
        qt = jnp.concatenate([q_ref[0]] * N_HEADS, axis=0)
        rr = lax.broadcasted_iota(jnp.int32, (rows, D_ATTN), 0) // t_new
        cc = lax.broadcasted_iota(jnp.int32, (rows, D_ATTN), 1) // HEAD_DIM
        qbd_scr[...] = jnp.where(rr == cc, qt, 0.0)
        gate_scr[...] = jnp.full((rows, LANES), -jnp.inf, F32)
        m_scr[...] = jnp.zeros((rows, LANES), F32)
        l_scr[...] = jnp.zeros((rows, LANES), F32)

    qbd = qbd_scr[...]
    qbd_bf = qbd.astype(BF16)
    gate_all, m_all, l_all = gate_scr[...], m_scr[...], l_scr[...]
    for j in range(blocks_per_step):
        n = step * blocks_per_step + j
        pages = slice(j * pages_per_block, (j + 1) * pages_per_block)
        kt = jnp.concatenate([r[0] for r in k_refs[pages]], axis=1)
        vt = jnp.concatenate([r[0] for r in v_refs[pages]], axis=1)
        kmean = jnp.sum(kt, axis=1, keepdims=True) * (1.0 / MOBA_BLOCK)
        gate_n = jnp.dot(qbd, kmean, precision=lax.Precision.HIGHEST, preferred_element_type=F32)
        s = jnp.dot(qbd_bf, kt.astype(BF16), preferred_element_type=F32) * scale
        pos = n * MOBA_BLOCK + lax.broadcasted_iota(jnp.int32, (rows, MOBA_BLOCK), 1)
        s = s - slope * ((past_len + t_of_r) - pos).astype(F32)
        m_n = jnp.max(s, axis=1, keepdims=True)
        p = jnp.exp(s - m_n)
        l_n = jnp.sum(p, axis=1, keepdims=True)
        o_scr[n] = _dot_nt(p.astype(BF16), vt.astype(BF16))
        here = lane == n
        gate_all = jnp.where(here, gate_n, gate_all)
        m_all = jnp.where(here, m_n, m_all)
        l_all = jnp.where(here, l_n, l_all)
    gate_scr[...] = gate_all
    m_scr[...] = m_all
    l_scr[...] = l_all

    @pl.when(step == n_steps - 1)
    def _():
        work = gate_all
        sel = jnp.zeros((rows, LANES), jnp.bool_)
        for _k in range(MOBA_TOPK):
            mx = jnp.max(work, axis=1, keepdims=True)
            idx = jnp.min(jnp.where(work == mx, lane, LANES), axis=1, keepdims=True)
            hit = lane == idx
            sel = sel | hit
            work = jnp.where(hit, -jnp.inf, work)
        kn = kn_ref[0]
        vn = vn_ref[0]
        sn = _dot_nt(qbd_bf, kn.astype(BF16)) * scale
        dn = t_of_r - lax.broadcasted_iota(jnp.int32, (rows, t_new), 1)
        sn = jnp.where(dn >= 0, sn - slope * dn.astype(F32), NEG_BIG)
        big_m = jnp.maximum(jnp.max(jnp.where(sel, m_all, NEG_BIG), axis=1, keepdims=True),
                            jnp.max(sn, axis=1, keepdims=True))
        w = jnp.where(sel, jnp.exp(jnp.where(sel, m_all, big_m) - big_m), 0.0)
        pn = jnp.exp(sn - big_m)
        l_tot = jnp.sum(w * l_all, axis=1, keepdims=True) + jnp.sum(pn, axis=1, keepdims=True)
        acc = jnp.dot(pn.astype(BF16), vn.astype(BF16), preferred_element_type=F32)
        for blk in range(o_scr.shape[0]):
            acc = acc + w[:, blk:blk + 1] * o_scr[blk]
        acc = acc / l_tot
        head_of_lane = lax.broadcasted_iota(jnp.int32, (t_new, D_ATTN), 1) // HEAD_DIM
        out = jnp.zeros((t_new, D_ATTN), F32)
        for h in range(N_HEADS):
            out = out + jnp.where(head_of_lane == h, acc[h * t_new:(h + 1) * t_new], 0.0)
        o_ref[0] = out


def _moba_sample(q, k_new, v_new, cache_kt_pages, cache_vt_pages, page_table_flat, page_base, n_pages, page_size):
    dec_batch, t_new, _ = q.shape
    pages_per_block = MOBA_BLOCK // page_size
    n_full = n_pages // pages_per_block
    blocks_per_step = MOBA_SAMPLE_BLOCKS_PER_STEP
    assert n_full * pages_per_block == n_pages and MOBA_TOPK <= n_full <= LANES and n_full % blocks_per_step == 0
    pages_per_step = blocks_per_step * pages_per_block
    rows = N_HEADS * t_new
    tok_spec = pl.BlockSpec((1, t_new, D_ATTN), lambda b, n, pt: (b, 0, 0))

    def page_spec(which):
        return pl.BlockSpec((1, D_ATTN, page_size),
                            lambda b, n, pt: (page_base + pt[b * n_pages + pages_per_step * n + which], 0, 0))

    page_specs = [page_spec(w) for w in range(pages_per_step)]
    grid_spec = pltpu.PrefetchScalarGridSpec(
        num_scalar_prefetch=1,
        grid=(dec_batch, n_full // blocks_per_step),
        in_specs=[tok_spec, tok_spec, tok_spec] + page_specs + page_specs,
        out_specs=pl.BlockSpec((1, t_new, D_ATTN), lambda b, n, pt: (b, 0, 0)),
        scratch_shapes=[
            pltpu.VMEM((rows, D_ATTN), F32),
            pltpu.VMEM((rows, LANES), F32),
            pltpu.VMEM((rows, LANES), F32),
            pltpu.VMEM((rows, LANES), F32),
            pltpu.VMEM((n_full, rows, D_ATTN), F32),
        ],
    )
    return pl.pallas_call(
        functools.partial(_moba_sample_kernel, past_len=n_pages * page_size, blocks_per_step=blocks_per_step,
                          pages_per_block=pages_per_block),
        grid_spec=grid_spec,
        out_shape=jax.ShapeDtypeStruct((dec_batch, t_new, D_ATTN), F32),
        compiler_params=_cparams(2),
        name="moba_sample",
    )(page_table_flat, q, k_new, v_new, *([cache_kt_pages] * pages_per_step), *([cache_vt_pages] * pages_per_step))


def _softplus(x):
    return jnp.maximum(x, 0.0) + jnp.log1p(jnp.exp(-jnp.abs(x)))


def _expm1(x):
    return jnp.tanh(0.5 * x) * (jnp.exp(x) + 1.0)


def _rglru_kernel(xr_ref, gr_ref, buf0_ref, h0_ref, cw_ref, cb_ref, wa_ref, ba_ref, wx_ref, bx_ref,
                  lam_ref, gain_ref, rnn_ref, conv_ref, hlast_ref, cx_scr, ch_scr):
    j = pl.program_id(1)
    nb, t, c = xr_ref.shape

    @pl.when(j == 0)
    def _():
        cx_scr[...] = buf0_ref[...]
        ch_scr[...] = h0_ref[...]

    x = xr_ref[...]
    ext = jnp.concatenate([cx_scr[...], x], axis=1)
    cw = cw_ref[...]
    xc = cb_ref[...]
    for tap in range(CONV_WIDTH - 1):
        shift = CONV_WIDTH - 1 - tap
        xc = xc + pltpu.roll(ext, shift, axis=1)[:, SUBLANES:, :] * cw[tap:tap + 1]
    xc = xc + x * cw[CONV_WIDTH - 1:CONV_WIDTH]

    x2 = xc.reshape(nb * t, c)
    xb = x2.astype(BF16)
    r = jax.nn.sigmoid(jnp.dot(xb, wa_ref[...], preferred_element_type=F32) + ba_ref[...])
    i = jax.nn.sigmoid(jnp.dot(xb, wx_ref[...], preferred_element_type=F32) + bx_ref[...])
    log_a = -LRU_C * r * _softplus(-lam_ref[...])
    a = jnp.exp(log_a)
    u = jnp.sqrt(-_expm1(2.0 * log_a)) * (i * x2)

    a3 = a.reshape(nb, t, c)
    b3 = u.reshape(nb, t, c)
    row = lax.broadcasted_iota(jnp.int32, (nb, t, c), 1)
    step = 1
    while step < t:
        a_prev = jnp.where(row >= step, pltpu.roll(a3, step, axis=1), 1.0)
        b_prev = jnp.where(row >= step, pltpu.roll(b3, step, axis=1), 0.0)
        b3 = a3 * b_prev + b3
        a3 = a3 * a_prev
        step *= 2
    h = a3 * ch_scr[...] + b3

    ch_scr[...] = h[:, t - 1:t, :]
    tail = x[:, t - SUBLANES:, :]
    cx_scr[...] = tail
    conv_ref[...] = tail
    hlast_ref[...] = h[:, t - SUBLANES:, :]

    rnn = h * _gelu(gr_ref[...])
    ms = jnp.mean(rnn * rnn, axis=-1, keepdims=True)
    rnn_ref[...] = (rnn * lax.rsqrt(ms + EPS) * gain_ref[...]).reshape(nb * t, c)


def _rglru(xr, gr, buf0, h0, conv_w, conv_b, wa_bd, ba, wx_bd, bx, lam, gain, nb, t):
    bt, tt, c = xr.shape
    g0, g1 = bt // nb, tt // t
    const = lambda shape: pl.BlockSpec(shape, lambda i, j: (0,) * len(shape))
    per_b = lambda r: pl.BlockSpec((nb, r, c), lambda i, j: (i, 0, 0))
    return pl.pallas_call(
        _rglru_kernel,
        grid=(g0, g1),
        in_specs=[
            pl.BlockSpec((nb, t, c), lambda i, j: (i, j, 0)),
            pl.BlockSpec((nb, t, c), lambda i, j: (i, j, 0)),
            per_b(SUBLANES),
            per_b(1),
            const((CONV_WIDTH, c)),
            const((1, c)),
            const((c, c)),
            const((1, c)),
            const((c, c)),
            const((1, c)),
            const((1, c)),
            const((1, c)),
        ],
        out_specs=[
            pl.BlockSpec((nb * t, c), lambda i, j: (i * g1 + j, 0)),
            per_b(SUBLANES),
            per_b(SUBLANES),
        ],
        out_shape=[
            jax.ShapeDtypeStruct((bt * tt, c), F32),
            jax.ShapeDtypeStruct((bt, SUBLANES, c), F32),
            jax.ShapeDtypeStruct((bt, SUBLANES, c), F32),
        ],
        scratch_shapes=[pltpu.VMEM((nb, SUBLANES, c), F32), pltpu.VMEM((nb, 1, c), F32)],
        compiler_params=_cparams(2),
        name="rglru",
    )(xr, gr, buf0, h0, conv_w, conv_b, wa_bd, ba, wx_bd, bx, lam, gain)


def _mix_kernel(attn_ref, rnn_ref, x_ref, mod_ref, ag_ref, wout_ref, nff_ref, wq_ref, x1_ref, h2_ref, qp_ref):
    nb, t, d = x_ref.shape
    attn = attn_ref[...]
    ms = jnp.mean(attn * attn, axis=-1, keepdims=True)
    attn_n = attn * lax.rsqrt(ms + EPS) * ag_ref[...]
    mixed = jnp.dot(attn_n.astype(BF16), wout_ref[0:D_ATTN, :], preferred_element_type=F32)
    mixed = mixed + jnp.dot(rnn_ref[...].astype(BF16), wout_ref[D_ATTN:, :], preferred_element_type=F32)
    mod = mod_ref[...]
    x1 = x_ref[...] + mod[:, 2:3, :] * mixed.reshape(nb, t, d)
    ms1 = jnp.mean(x1 * x1, axis=-1, keepdims=True)
    h = x1 * lax.rsqrt(ms1 + EPS) * nff_ref[...]
    h = h * (1.0 + mod[:, 4:5, :]) + mod[:, 3:4, :]
    h2 = h.reshape(nb * t, d).astype(BF16)
    x1_ref[...] = x1
    h2_ref[...] = h2
    qp = jnp.dot(h2, wq_ref[...], preferred_element_type=F32).astype(BF16)
    for hp in range(qp_ref.shape[0]):
        qp_ref[hp] = qp[:, hp * PEER_NKEYS:(hp + 1) * PEER_NKEYS]


def _mix(attn, rnn_n, x, mod, attn_gain, w_out_bf, norm_ff, wq_bf, nb, t):
    bt, tt, d = x.shape
    g0, g1 = bt // nb, tt // t
    rows = nb * t
    n_rows = bt * tt
    d_rnn = rnn_n.shape[1]
    dq = wq_bf.shape[1]
    const = lambda shape: pl.BlockSpec(shape, lambda i, j: (0,) * len(shape))
    row_spec = lambda w: pl.BlockSpec((rows, w), lambda i, j: (i * g1 + j, 0))
    return pl.pallas_call(
        _mix_kernel,
        grid=(g0, g1),
        in_specs=[
            row_spec(D_ATTN),
            row_spec(d_rnn),
            pl.BlockSpec((nb, t, d), lambda i, j: (i, j, 0)),
            pl.BlockSpec((nb, N_MOD, d), lambda i, j: (i, 0, 0)),
            const((1, D_ATTN)),
            const((D_ATTN + d_rnn, d)),
            const((1, d)),
            const((d, dq)),
        ],
        out_specs=[pl.BlockSpec((nb, t, d), lambda i, j: (i, j, 0)), row_spec(d),
                   pl.BlockSpec((dq // PEER_NKEYS, rows, PEER_NKEYS), lambda i, j: (0, i * g1 + j, 0))],
        out_shape=[
            jax.ShapeDtypeStruct((bt, tt, d), F32),
            jax.ShapeDtypeStruct((n_rows, d), BF16),
            jax.ShapeDtypeStruct((dq // PEER_NKEYS, n_rows, PEER_NKEYS), BF16),
        ],
        compiler_params=_cparams(2),
        name="mix",
    )(attn, rnn_n, x, mod, attn_gain, w_out_bf, norm_ff, wq_bf)


def _extract_topk(work, k):
    n_rows = work.shape[0]
    rid = lax.broadcasted_iota(jnp.int32, work.shape, 0).astype(F32)
    rank = jnp.full(work.shape, RANK_BIG, F32)
    vals = []
    for step in range(k):
        mx = jnp.max(work, axis=0, keepdims=True)
        idx = jnp.min(jnp.where(work == mx, rid, float(n_rows)), axis=0, keepdims=True)
        hit = rid == idx
        rank = jnp.where(hit, float(step), rank)
        work = jnp.where(hit, -jnp.inf, work)
        vals.append(mx)
    return jnp.concatenate(vals, axis=0), rank


def _staircase_sums(a, b):
    k = a.shape[0]
    row = lax.broadcasted_iota(jnp.int32, (SUBLANES, a.shape[1]), 0)
    pieces = [a[0:1] + b]
    for k0 in range(1, SUBLANES):
        n1 = k // (k0 + 1)
        blk = a[k0:k0 + 1] + b[0:SUBLANES]
        pieces.append(blk if n1 >= SUBLANES else jnp.where(row < n1, blk, -jnp.inf))
    for k0 in range(SUBLANES, k, SUBLANES):
        pieces.append(a[k0:k0 + SUBLANES] + b[0:1])
    return jnp.concatenate(pieces, axis=0)


def _kth_largest(work, k):
    kf = float(k)
    cum = jnp.zeros((1, work.shape[1]), F32)
    tau = cum
    z = cum
    top = None
    for _ in range(k):
        mx = jnp.max(work, axis=0, keepdims=True)
        top = mx if top is None else top
        eq = work == mx
        cnt = jnp.sum(jnp.where(eq, 1.0, 0.0), axis=0, keepdims=True)
        still = cum < kf
        tau = jnp.where(still, mx, tau)
        z = z + jnp.where(still, cnt * jnp.exp(mx - top), 0.0)
        cum = jnp.where(still, cum + cnt, cum)
        work = jnp.where(eq, -jnp.inf, work)
    return tau, cum, z


def _pair_sums(sv0, sv1):
    return jnp.concatenate([sv0[k:k + 1] + sv1 for k in range(sv0.shape[0])], axis=0)


def _peer_kernel(x1_ref, mod_ref, h2_ref, qp_ref, keys_ref, u_ref, v_ref, y_ref,
                 sm_scr, e_scr, r_scr, sv_scr, tau_scr, ptau_scr, st_scr, a_scr, wt_scr, acc_scr, flag_ref):
    et = pl.program_id(2)
    n_et = pl.num_programs(2)
    nb, t, d = x1_ref.shape
    c_tok = nb * t
    n_chunks = c_tok // LANES
    te = u_ref.shape[0]
    i_per_tile = te // PEER_NKEYS
    k_top = float(PEER_TOPK)

    def chunk(c):
        return slice(c * LANES, (c + 1) * LANES)

    def scores(hp):
        st_scr[...] = _dot_nt(keys_ref[hp], qp_ref[hp])

    def select_fast():
        def half_body(hp, bad):
            scores(hp)
            for c in range(n_chunks):
                s = st_scr[:, chunk(c)]
                sv, rank = _extract_topk(s, PEER_TOPK)
                member = rank < k_top
                sv_scr[hp, c] = sv
                sm_scr[hp, c] = jnp.where(member, s, -jnp.inf)
                e_scr[hp, c] = jnp.where(member, jnp.exp(s - sv[0:1]), 0.0)
            return bad

        bad = lax.fori_loop(0, 2 * PEER_HEADS, half_body, jnp.zeros((1, LANES), F32))

        def head_body(h, bad):
            for c in range(n_chunks):
                cand = _staircase_sums(sv_scr[2 * h, c], sv_scr[2 * h + 1, c])
                tau, n_ge, z = _kth_largest(cand, PEER_TOPK)
                bad = jnp.maximum(bad, jnp.where(n_ge != k_top, 1.0, 0.0))
                tau_scr[h, c] = tau
                e_scr[2 * h, c] = e_scr[2 * h, c] / z
            return bad

        return lax.fori_loop(0, PEER_HEADS, head_body, bad)

    def select_exact():
        def half_body(hp, carry):
            scores(hp)
            for c in range(n_chunks):
                s = st_scr[:, chunk(c)]
                sv, rank = _extract_topk(s, PEER_TOPK)
                member = rank < k_top
                sv_scr[hp, c] = sv
                sm_scr[hp, c] = jnp.where(member, s, -jnp.inf)
                e_scr[hp, c] = jnp.where(member, jnp.exp(s - sv[0:1]), 0.0)
                r_scr[hp, c] = rank
            return carry

        lax.fori_loop(0, 2 * PEER_HEADS, half_body, 0)

        def head_body(h, carry):
            for c in range(n_chunks):
                cand = _pair_sums(sv_scr[2 * h, c], sv_scr[2 * h + 1, c])
                fv, crank = _extract_topk(cand, PEER_TOPK)
                z = jnp.sum(jnp.exp(fv - fv[0:1]), axis=0, keepdims=True)
                pos = lax.broadcasted_iota(jnp.int32, cand.shape, 0).astype(F32)
                tau_scr[h, c] = fv[PEER_TOPK - 1:PEER_TOPK]
                ptau_scr[h, c] = jnp.sum(jnp.where(crank == k_top - 1.0, pos, 0.0), axis=0, keepdims=True)
                e_scr[2 * h, c] = e_scr[2 * h, c] / z
                r_scr[2 * h, c] = r_scr[2 * h, c] * k_top
            return carry

        lax.fori_loop(0, PEER_HEADS, head_body, 0)

    @pl.when(et == 0)
    def _():
        bad = select_fast()
        flag_ref[0] = jnp.max(bad).astype(jnp.int32)
        acc_scr[...] = jnp.zeros(acc_scr.shape, F32)

    @pl.when((et == 0) & (flag_ref[0] != 0))
    def _():
        select_exact()

    a_scr[...] = _dot_nt(u_ref[...], h2_ref[...])

    def weights(exact_ties):
        def body(ii, carry):
            i = et * i_per_tile + ii
            rows = pl.ds(pl.multiple_of(ii * PEER_NKEYS, PEER_NKEYS), PEER_NKEYS)
            for c in range(n_chunks):
                g = jnp.zeros((PEER_NKEYS, LANES), F32)
                for h in range(PEER_HEADS):
                    tsum = sm_scr[2 * h + 1, c] + sm_scr[2 * h, c, pl.ds(i, 1), :]
                    val = e_scr[2 * h + 1, c] * e_scr[2 * h, c, pl.ds(i, 1), :]
                    tau = tau_scr[h, c]
                    if exact_ties:
                        pos = r_scr[2 * h + 1, c] + r_scr[2 * h, c, pl.ds(i, 1), :]
                        keep = (tsum > tau) | ((tsum == tau) & (pos <= ptau_scr[h, c]))
                    else:
                        keep = tsum >= tau
                    g = g + jnp.where(keep, val, 0.0)
                wt_scr[rows, chunk(c)] = (g * _gelu(a_scr[rows, chunk(c)])).astype(BF16)
            return carry

        lax.fori_loop(0, i_per_tile, body, 0)

    @pl.when(flag_ref[0] == 0)
    def _():
        weights(False)

    @pl.when(flag_ref[0] != 0)
    def _():
        weights(True)

    acc_scr[...] += jnp.dot(v_ref[...], wt_scr[...], preferred_element_type=F32)

    @pl.when(et == n_et - 1)
    def _():
        y_ref[...] = x1_ref[...] + mod_ref[...][:, 5:6, :] * acc_scr[...].T.reshape(nb, t, d)


def _peer(x1, mod, h2, qp, keys_bf, u_bf, vt_bf, nb, t):
    bt, tt, d = x1.shape
    g0, g1 = bt // nb, tt // t
    c_tok = nb * t
    n_chunks = c_tok // LANES
    n_exp = u_bf.shape[0]
    te = PEER_EXPERT_TILE
    n_et = n_exp // te
    n_hp = qp.shape[0]
    tok3 = pl.BlockSpec((nb, t, d), lambda i, j, e: (i, j, 0))
    halves = lambda r: pltpu.VMEM((n_hp, n_chunks, r, LANES), F32)
    heads = lambda r: pltpu.VMEM((PEER_HEADS, n_chunks, r, LANES), F32)
    return pl.pallas_call(
        _peer_kernel,
        grid=(g0, g1, n_et),
        in_specs=[
            tok3,
            pl.BlockSpec((nb, N_MOD, d), lambda i, j, e: (i, 0, 0)),
            pl.BlockSpec((c_tok, d), lambda i, j, e: (i * g1 + j, 0)),
            pl.BlockSpec((n_hp, c_tok, PEER_NKEYS), lambda i, j, e: (0, i * g1 + j, 0)),
            pl.BlockSpec(keys_bf.shape, lambda i, j, e: (0, 0, 0)),
            pl.BlockSpec((te, d), lambda i, j, e: (e, 0)),
            pl.BlockSpec((d, te), lambda i, j, e: (0, e)),
        ],
        out_specs=tok3,
        out_shape=jax.ShapeDtypeStruct((bt, tt, d), F32),
        scratch_shapes=[
            halves(PEER_NKEYS), halves(PEER_NKEYS), halves(PEER_NKEYS), halves(PEER_TOPK),
            heads(1), heads(1),
            pltpu.VMEM((PEER_NKEYS, c_tok), F32),
            pltpu.VMEM((te, c_tok), F32),
            pltpu.VMEM((te, c_tok), BF16),
            pltpu.VMEM((d, c_tok), F32),
            pltpu.SMEM((1,), jnp.int32),
        ],
        compiler_params=_cparams(3),
        name="peer",
    )(x1, mod, h2, qp, keys_bf, u_bf, vt_bf)


def _block_diag(w):
    n, c, d = w.shape
    eye = jnp.eye(n, dtype=w.dtype)
    return (eye[:, None, :, None] * w[:, :, None, :]).reshape(n * c, n * d)


def kernel(x_prompt, x_sample, cache_k, cache_v, state_conv, state_h, page_table, c_prompt, c_sample, w_ada, b_ada, norm_mix, norm_ff, w_in, q_gain, k_gain, conv_w, conv_b, lru_wa, lru_ba, lru_wx, lru_bx, lru_lambda, attn_out_gain, rnn_out_gain, w_out, peer_wq, peer_keys, peer_u, peer_v):
    depth = w_ada.shape[0]
    batch, seq, d_model = x_prompt.shape
    dec_batch, dec_seq, _ = x_sample.shape
    _, n_pool, page_size, n_heads, head_dim = cache_k.shape
    n_pages = page_table.shape[1]
    d_rnn = d_model - D_ATTN
    assert (n_heads, head_dim) == (N_HEADS, HEAD_DIM)
    assert seq % ROW_TILE == 0 and ROW_TILE % MOBA_BLOCK == 0 and ROW_TILE % dec_seq == 0
    assert dec_seq == SUBLANES and (dec_batch * dec_seq) % ROW_TILE == 0
    assert (n_pages * page_size) % MOBA_BLOCK == 0

    head_mean = _block_diag(jnp.full((N_HEADS, HEAD_DIM, HEAD_DIM), 1.0 / HEAD_DIM, F32)).astype(BF16)
    cache_k_pages = cache_k.transpose(0, 1, 3, 4, 2).reshape(depth * n_pool, D_ATTN, page_size)
    cache_v_pages = cache_v.transpose(0, 1, 3, 4, 2).reshape(depth * n_pool, D_ATTN, page_size)
    page_table_flat = page_table.reshape(-1).astype(jnp.int32)
    nb_s = ROW_TILE // dec_seq

    yp, ys = x_prompt, x_sample
    outs = [[] for _ in range(8)]
    for l in range(depth):
        mod = _ada(jnp.concatenate([c_prompt, c_sample], axis=0), w_ada[l], b_ada[l])
        mod = mod.reshape(batch + dec_batch, N_MOD, d_model)
        mod_p, mod_s = mod[:batch], mod[batch:]
        w_in_bf = w_in[l].astype(BF16)
        w_out_bf = w_out[l].astype(BF16)
        wq_bf = peer_wq[l].astype(BF16)
        keys_bf = peer_keys[l].reshape(2 * PEER_HEADS, PEER_NKEYS, -1).astype(BF16)
        u_bf = peer_u[l].astype(BF16)
        vt_bf = peer_v[l].T.astype(BF16)
        wa_bd = _block_diag(lru_wa[l]).astype(BF16)
        wx_bd = _block_diag(lru_wx[l]).astype(BF16)
        row = lambda a: a.reshape(1, -1)
        q_gain_t = row(jnp.tile(q_gain[l], N_HEADS))
        k_gain_t = row(jnp.tile(k_gain[l], N_HEADS))
        rglru_w = (conv_w[l], row(conv_b[l]), wa_bd, row(lru_ba[l]), wx_bd, row(lru_bx[l]),
                   row(lru_lambda[l]), row(rnn_out_gain[l]))

        q, k, v, xr, gr, kbf, vbf, ksum = _proj(yp, mod_p, row(norm_mix[l]), w_in_bf, head_mean, q_gain_t, k_gain_t,
                                                1, ROW_TILE, True)
        attn = _moba_prompt(q, kbf, vbf, ksum.reshape(batch, seq // MOBA_BLOCK, D_ATTN), batch, seq)
        rnn_n, conv_p, h_p = _rglru(xr.reshape(batch, seq, d_rnn), gr.reshape(batch, seq, d_rnn),
                                    jnp.zeros((batch, SUBLANES, d_rnn), F32), jnp.zeros((batch, 1, d_rnn), F32),
                                    *rglru_w, 1, ROW_TILE)
        x1, h2, qp = _mix(attn, rnn_n, yp, mod_p, row(attn_out_gain[l]), w_out_bf, row(norm_ff[l]), wq_bf, 1, ROW_TILE)
        yp = _peer(x1, mod_p, h2, qp, keys_bf, u_bf, vt_bf, 1, ROW_TILE)
        outs[0].append(k.reshape(batch, seq, N_HEADS, HEAD_DIM))
        outs[1].append(v.reshape(batch, seq, N_HEADS, HEAD_DIM))
        outs[2].append(conv_p[:, SUBLANES - (CONV_WIDTH - 1):])
        outs[3].append(h_p[:, SUBLANES - 1])

        q, k, v, xr, gr = _proj(ys, mod_s, row(norm_mix[l]), w_in_bf, head_mean, q_gain_t, k_gain_t,
                                nb_s, dec_seq, False)
        tok3 = lambda a: a.reshape(dec_batch, dec_seq, -1)
        attn = _moba_sample(tok3(q), tok3(k), tok3(v), cache_k_pages, cache_v_pages, page_table_flat,
                            l * n_pool, n_pages, page_size)
        buf0 = jnp.pad(state_conv[l], ((0, 0), (SUBLANES - (CONV_WIDTH - 1), 0), (0, 0)))
        rnn_n, conv_s, h_s = _rglru(tok3(xr), tok3(gr), buf0, state_h[l][:, None, :], *rglru_w, nb_s, dec_seq)
        x1, h2, qp = _mix(attn.reshape(dec_batch * dec_seq, D_ATTN), rnn_n, ys, mod_s, row(attn_out_gain[l]),
                          w_out_bf, row(norm_ff[l]), wq_bf, nb_s, dec_seq)
        ys = _peer(x1, mod_s, h2, qp, keys_bf, u_bf, vt_bf, nb_s, dec_seq)
        outs[4].append(k.reshape(dec_batch, dec_seq, N_HEADS, HEAD_DIM))
        outs[5].append(v.reshape(dec_batch, dec_seq, N_HEADS, HEAD_DIM))
        outs[6].append(conv_s[:, SUBLANES - (CONV_WIDTH - 1):])
        outs[7].append(h_s[:, SUBLANES - 1])
    return (yp, ys) + tuple(jnp.stack(o) for o in outs)
```

```python
import functools

import jax
import jax.numpy as jnp
from jax import lax
from jax.experimental import pallas as pl
from jax.experimental.pallas import tpu as pltpu

F32 = jnp.float32
BF16 = jnp.bfloat16

N_HEADS = 8
HEAD_DIM = 64
D_ATTN = N_HEADS * HEAD_DIM
CONV_WIDTH = 4
LRU_C = 8.0
MOBA_BLOCK = 256
MOBA_TOPK = 3
PEER_HEADS = 8
PEER_NKEYS = 128
PEER_TOPK = 16
N_MOD = 6
EPS = 1e-6

ROW_TILE = 512
PEER_EXPERT_TILE = 1024
PEER_SELECT_LANES = 512
MOBA_SAMPLE_BLOCKS_PER_STEP = 8
SUBLANES = 8
LANES = 128
NEG_BIG = -1e30
RANK_BIG = 1e6
VMEM_LIMIT = 48 * 1024 * 1024


def _cparams(n_axes, vmem=VMEM_LIMIT):
    return pltpu.CompilerParams(dimension_semantics=("arbitrary",) * n_axes, vmem_limit_bytes=vmem)


def _dot_nt(a, b, precision=None):
    return lax.dot_general(a, b, (((1,), (1,)), ((), ())), precision=precision, preferred_element_type=F32)


def _gelu(x):
    return jax.nn.gelu(x)


def _gelu_tanh(x):
    c = 0.7978845608028654
    half = 0.5 * x
    return half + half * jnp.tanh(x * (c + (c * 0.044715) * (x * x)))


def _ada_kernel(c_ref, w_ref, b_ref, o_ref):
    c = c_ref[...]
    s = c * jax.nn.sigmoid(c)
    o_ref[...] = jnp.dot(s.astype(BF16), w_ref[...].astype(BF16), preferred_element_type=F32) + b_ref[...]


def _ada(c_all, w_ada, b_ada):
    n, d = c_all.shape
    n_out = w_ada.shape[1]
    tile = d
    return pl.pallas_call(
        _ada_kernel,
        grid=(n_out // tile,),
        in_specs=[
            pl.BlockSpec((n, d), lambda j: (0, 0)),
            pl.BlockSpec((d, tile), lambda j: (0, j)),
            pl.BlockSpec((1, tile), lambda j: (0, j)),
        ],
        out_specs=pl.BlockSpec((n, tile), lambda j: (0, j)),
        out_shape=jax.ShapeDtypeStruct((n, n_out), F32),
        compiler_params=_cparams(1),
        name="ada",
    )(c_all, w_ada, b_ada.reshape(1, n_out))


def _proj_kernel(x_ref, mod_ref, nw_ref, win_ref, hm_ref, qg_ref, kg_ref, *out_refs, attn_extras):
    nb, t, d = x_ref.shape
    x = x_ref[...]
    mod = mod_ref[...]
    ms = jnp.mean(x * x, axis=-1, keepdims=True)
    h = x * lax.rsqrt(ms + EPS) * nw_ref[...]
    h = h * (1.0 + mod[:, 1:2, :]) + mod[:, 0:1, :]
    proj = jnp.dot(h.reshape(nb * t, d).astype(BF16), win_ref[...], preferred_element_type=F32)
    q = proj[:, 0:D_ATTN]
    k = proj[:, D_ATTN:2 * D_ATTN]
    v = proj[:, 2 * D_ATTN:3 * D_ATTN]
    d_rnn = (proj.shape[1] - 3 * D_ATTN) // 2
    xr = proj[:, 3 * D_ATTN:3 * D_ATTN + d_rnn]
    gr = proj[:, 3 * D_ATTN + d_rnn:]
    hm = hm_ref[...]

    def head_norm(a, gain):
        msq = jnp.dot((a * a).astype(BF16), hm, preferred_element_type=F32)
        return a * lax.rsqrt(msq + EPS) * gain

    q = head_norm(q, qg_ref[...])
    k = head_norm(k, kg_ref[...])
    q_ref, k_ref, v_ref, xr_ref, gr_ref = out_refs[:5]
    q_ref[...] = q
    k_ref[...] = k
    v_ref[...] = v
    xr_ref[...] = xr
    gr_ref[...] = gr
    if attn_extras:
        kbf_ref, vbf_ref, ksum_ref = out_refs[5:]
        kbf_ref[...] = k.astype(BF16)
        vbf_ref[...] = v.astype(BF16)
        n_blk = (nb * t) // MOBA_BLOCK
        sums = [jnp.sum(k[i * MOBA_BLOCK:(i + 1) * MOBA_BLOCK], axis=0, keepdims=True) for i in range(n_blk)]
        ksum_ref[0] = jnp.concatenate(sums, axis=0)


def _proj(x, mod, norm_w, w_in_bf, head_mean, q_gain_t, k_gain_t, nb, t, attn_extras):
    bt, tt, d = x.shape
    g0, g1 = bt // nb, tt // t
    rows = nb * t
    n_rows = bt * tt
    d_in = w_in_bf.shape[1]
    d_rnn = (d_in - 3 * D_ATTN) // 2
    row_spec = lambda w: pl.BlockSpec((rows, w), lambda i, j: (i * g1 + j, 0))
    const = lambda shape: pl.BlockSpec(shape, lambda i, j: (0,) * len(shape))
    out_specs = [row_spec(D_ATTN)] * 3 + [row_spec(d_rnn)] * 2
    out_shape = [jax.ShapeDtypeStruct((n_rows, D_ATTN), F32)] * 3 + [jax.ShapeDtypeStruct((n_rows, d_rnn), F32)] * 2
    if attn_extras:
        n_blk = rows // MOBA_BLOCK
        out_specs += [row_spec(D_ATTN)] * 2 + [pl.BlockSpec((1, n_blk, D_ATTN), lambda i, j: (i * g1 + j, 0, 0))]
        out_shape += [jax.ShapeDtypeStruct((n_rows, D_ATTN), BF16)] * 2
        out_shape += [jax.ShapeDtypeStruct((g0 * g1, n_blk, D_ATTN), F32)]
    return pl.pallas_call(
        functools.partial(_proj_kernel, attn_extras=attn_extras),
        grid=(g0, g1),
        in_specs=[
            pl.BlockSpec((nb, t, d), lambda i, j: (i, j, 0)),
            pl.BlockSpec((nb, N_MOD, d), lambda i, j: (i, 0, 0)),
            const((1, d)),
            const((d, d_in)),
            const((D_ATTN, D_ATTN)),
            const((1, D_ATTN)),
            const((1, D_ATTN)),
        ],
        out_specs=out_specs,
        out_shape=out_shape,
        compiler_params=_cparams(2),
        name="proj",
    )(x, mod, norm_w, w_in_bf, head_mean, q_gain_t, k_gain_t)


def _moba_prompt_kernel(q_ref, k_ref, v_ref, ksum_ref, o_ref):
    qb = pl.program_id(1)
    tq = q_ref.shape[0]
    s_len = k_ref.shape[0]
    n_blk = s_len // MOBA_BLOCK
    q = q_ref[...]
    kall = k_ref[...]
    vall = v_ref[...]
    kmean = ksum_ref[0] * (1.0 / MOBA_BLOCK)
    row = lax.broadcasted_iota(jnp.int32, (tq, s_len), 0)
    col = lax.broadcasted_iota(jnp.int32, (tq, s_len), 1)
    t_pos = qb * tq + row
    dist = (t_pos - col).astype(F32)
    own = ((col // MOBA_BLOCK) == qb) & (col <= t_pos)
    blk_of_col = lax.broadcasted_iota(jnp.int32, (n_blk, s_len), 1) // MOBA_BLOCK
    expand = (blk_of_col == lax.broadcasted_iota(jnp.int32, (n_blk, s_len), 0)).astype(BF16)
    n_iota = lax.broadcasted_iota(jnp.int32, (tq, n_blk), 1)
    outs = []
    for h in range(N_HEADS):
        hs = slice(h * HEAD_DIM, (h + 1) * HEAD_DIM)
        qh = q[:, hs]
        gate = _dot_nt(qh, kmean[:, hs], precision=lax.Precision.HIGHEST)
        cnt = jnp.zeros((tq, n_blk), jnp.int32)
        for m in range(n_blk):
            gm = gate[:, m:m + 1]
            beats = (gm > gate) | ((gm == gate) & (m < n_iota))
            cnt = cnt + jnp.where(beats, (m < qb).astype(jnp.int32), 0)
        sel = ((cnt < MOBA_TOPK) & (n_iota < qb)).astype(BF16)
        sel_cols = jnp.dot(sel, expand, preferred_element_type=F32)
        mask = own | (sel_cols > 0.5)
        slope = 2.0 ** (-8.0 * (h + 1) / N_HEADS)
        s = _dot_nt(qh.astype(BF16), kall[:, hs]) * (HEAD_DIM ** -0.5) - slope * dist
        s = jnp.where(mask, s, NEG_BIG)
        mx = jnp.max(s, axis=-1, keepdims=True)
        p = jnp.exp(s - mx)
        l = jnp.sum(p, axis=-1, keepdims=True)
        o = jnp.dot(p.astype(BF16), vall[:, hs], preferred_element_type=F32)
        outs.append(o / l)
    o_ref[...] = jnp.concatenate(outs, axis=1)


def _moba_prompt(q, kbf, vbf, ksum, batch, seq):
    nq = seq // MOBA_BLOCK
    return pl.pallas_call(
        _moba_prompt_kernel,
        grid=(batch, nq),
        in_specs=[
            pl.BlockSpec((MOBA_BLOCK, D_ATTN), lambda b, i: (b * nq + i, 0)),
            pl.BlockSpec((seq, D_ATTN), lambda b, i: (b, 0)),
            pl.BlockSpec((seq, D_ATTN), lambda b, i: (b, 0)),
            pl.BlockSpec((1, nq, D_ATTN), lambda b, i: (b, 0, 0)),
        ],
        out_specs=pl.BlockSpec((MOBA_BLOCK, D_ATTN), lambda b, i: (b * nq + i, 0)),
        out_shape=jax.ShapeDtypeStruct((batch * seq, D_ATTN), F32),
        compiler_params=_cparams(2),
        name="moba_prompt",
    )(q, kbf, vbf, ksum)


def _moba_sample_kernel(pt_ref, q_ref, kn_ref, vn_ref, *refs, past_len, blocks_per_step, pages_per_block):
    del pt_ref
    n_pg = blocks_per_step * pages_per_block
    k_refs, v_refs = refs[:n_pg], refs[n_pg:2 * n_pg]
    o_ref, qbd_scr, gate_scr, m_scr, l_scr, o_scr = refs[2 * n_pg:]
    step = pl.program_id(1)
    n_steps = pl.num_programs(1)
    t_new = q_ref.shape[1]
    rows = N_HEADS * t_new
    r_iota = lax.broadcasted_iota(jnp.int32, (rows, 1), 0)
    t_of_r = r_iota % t_new
    h_of_r = r_iota // t_new
    slope = jnp.exp2(-8.0 * (h_of_r + 1).astype(F32) / N_HEADS)
    lane = lax.broadcasted_iota(jnp.int32, (rows, LANES), 1)
    scale = HEAD_DIM ** -0.5

    @pl.when(step == 0)
    def _():
        qt = jnp.concatenate([q_ref[0]] * N_HEADS, axis=0)
        rr = lax.broadcasted_iota(jnp.int32, (rows, D_ATTN), 0) // t_new
        cc = lax.broadcasted_iota(jnp.int32, (rows, D_ATTN), 1) // HEAD_DIM
        qbd_scr[...] = jnp.where(rr == cc, qt, 0.0)
        gate_scr[...] = jnp.full((rows, LANES), -jnp.inf, F32)
        m_scr[...] = jnp.zeros((rows, LANES), F32)
        l_scr[...] = jnp.zeros((rows, LANES), F32)

    qbd = qbd_scr[...]
    qbd_bf = qbd.astype(BF16)
    gate_all, m_all, l_all = gate_scr[...], m_scr[...], l_scr[...]
    for j in range(blocks_per_step):
        n = step * blocks_per_step + j
        pages = slice(j * pages_per_block, (j + 1) * pages_per_block)
        kt = jnp.concatenate([r[0] for r in k_refs[pages]], axis=1)
        vt = jnp.concatenate([r[0] for r in v_refs[pages]], axis=1)
        kmean = jnp.sum(kt, axis=1, keepdims=True) * (1.0 / MOBA_BLOCK)
        gate_n = jnp.dot(qbd, kmean, precision=lax.Precision.HIGHEST, preferred_element_type=F32)
        s = jnp.dot(qbd_bf, kt.astype(BF16), preferred_element_type=F32) * scale
        pos = n * MOBA_BLOCK + lax.broadcasted_iota(jnp.int32, (rows, MOBA_BLOCK), 1)
        s = s - slope * ((past_len + t_of_r) - pos).astype(F32)
        m_n = jnp.max(s, axis=1, keepdims=True)
        p = jnp.exp(s - m_n)
        l_n = jnp.sum(p, axis=1, keepdims=True)
        o_scr[n] = _dot_nt(p.astype(BF16), vt.astype(BF16))
        here = lane == n
        gate_all = jnp.where(here, gate_n, gate_all)
        m_all = jnp.where(here, m_n, m_all)
        l_all = jnp.where(here, l_n, l_all)
    gate_scr[...] = gate_all
    m_scr[...] = m_all
    l_scr[...] = l_all

    @pl.when(step == n_steps - 1)
    def _():
        work = gate_all
        sel = jnp.zeros((rows, LANES), jnp.bool_)
        for _k in range(MOBA_TOPK):
            mx = jnp.max(work, axis=1, keepdims=True)
            idx = jnp.min(jnp.where(work == mx, lane, LANES), axis=1, keepdims=True)
            hit = lane == idx
            sel = sel | hit
            work = jnp.where(hit, -jnp.inf, work)
        kn = kn_ref[0]
        vn = vn_ref[0]
        sn = _dot_nt(qbd_bf, kn.astype(BF16)) * scale
        dn = t_of_r - lax.broadcasted_iota(jnp.int32, (rows, t_new), 1)
        sn = jnp.where(dn >= 0, sn - slope * dn.astype(F32), NEG_BIG)
        big_m = jnp.maximum(jnp.max(jnp.where(sel, m_all, NEG_BIG), axis=1, keepdims=True),
                            jnp.max(sn, axis=1, keepdims=True))
        w = jnp.where(sel, jnp.exp(jnp.where(sel, m_all, big_m) - big_m), 0.0)
        pn = jnp.exp(sn - big_m)
        l_tot = jnp.sum(w * l_all, axis=1, keepdims=True) + jnp.sum(pn, axis=1, keepdims=True)
        acc = jnp.dot(pn.astype(BF16), vn.astype(BF16), preferred_element_type=F32)
        for blk in range(o_scr.shape[0]):
            acc = acc + w[:, blk:blk + 1] * o_scr[blk]
        acc = acc / l_tot
        head_of_lane = lax.broadcasted_iota(jnp.int32, (t_new, D_ATTN), 1) // HEAD_DIM
        out = jnp.zeros((t_new, D_ATTN), F32)
        for h in range(N_HEADS):
            out = out + jnp.where(head_of_lane == h, acc[h * t_new:(h + 1) * t_new], 0.0)
        o_ref[0] = out


def _moba_sample(q, k_new, v_new, cache_kt_pages, cache_vt_pages, page_table_flat, page_base, n_pages, page_size):
    dec_batch, t_new, _ = q.shape
    pages_per_block = MOBA_BLOCK // page_size
    n_full = n_pages // pages_per_block
    blocks_per_step = MOBA_SAMPLE_BLOCKS_PER_STEP
    assert n_full * pages_per_block == n_pages and MOBA_TOPK <= n_full <= LANES and n_full % blocks_per_step == 0
    pages_per_step = blocks_per_step * pages_per_block
    rows = N_HEADS * t_new
    tok_spec = pl.BlockSpec((1, t_new, D_ATTN), lambda b, n, pt: (b, 0, 0))

    def page_spec(which):
        return pl.BlockSpec((1, D_ATTN, page_size),
                            lambda b, n, pt: (page_base + pt[b * n_pages + pages_per_step * n + which], 0, 0))

    page_specs = [page_spec(w) for w in range(pages_per_step)]
    grid_spec = pltpu.PrefetchScalarGridSpec(
        num_scalar_prefetch=1,
        grid=(dec_batch, n_full // blocks_per_step),
        in_specs=[tok_spec, tok_spec, tok_spec] + page_specs + page_specs,
        out_specs=pl.BlockSpec((1, t_new, D_ATTN), lambda b, n, pt: (b, 0, 0)),
        scratch_shapes=[
            pltpu.VMEM((rows, D_ATTN), F32),
            pltpu.VMEM((rows, LANES), F32),
            pltpu.VMEM((rows, LANES), F32),
            pltpu.VMEM((rows, LANES), F32),
            pltpu.VMEM((n_full, rows, D_ATTN), F32),
        ],
    )
    return pl.pallas_call(
        functools.partial(_moba_sample_kernel, past_len=n_pages * page_size, blocks_per_step=blocks_per_step,
                          pages_per_block=pages_per_block),
        grid_spec=grid_spec,
        out_shape=jax.ShapeDtypeStruct((dec_batch, t_new, D_ATTN), F32),
        compiler_params=_cparams(2),
        name="moba_sample",
    )(page_table_flat, q, k_new, v_new, *([cache_kt_pages] * pages_per_step), *([cache_vt_pages] * pages_per_step))


def _softplus(x):
    return jnp.maximum(x, 0.0) + jnp.log1p(jnp.exp(-jnp.abs(x)))


def _expm1(x):
    return jnp.tanh(0.5 * x) * (jnp.exp(x) + 1.0)


def _rglru_kernel(xr_ref, gr_ref, buf0_ref, h0_ref, cw_ref, cb_ref, wa_ref, ba_ref, wx_ref, bx_ref,
                  lam_ref, gain_ref, rnn_ref, conv_ref, hlast_ref, cx_scr, ch_scr):
    j = pl.program_id(1)
    nb, t, c = xr_ref.shape

    @pl.when(j == 0)
    def _():
        cx_scr[...] = buf0_ref[...]
        ch_scr[...] = h0_ref[...]

    x = xr_ref[...]
    ext = jnp.concatenate([cx_scr[...], x], axis=1)
    cw = cw_ref[...]
    xc = cb_ref[...]
    for tap in range(CONV_WIDTH - 1):
        shift = CONV_WIDTH - 1 - tap
        xc = xc + pltpu.roll(ext, shift, axis=1)[:, SUBLANES:, :] * cw[tap:tap + 1]
    xc = xc + x * cw[CONV_WIDTH - 1:CONV_WIDTH]

    x2 = xc.reshape(nb * t, c)
    xb = x2.astype(BF16)
    r = jax.nn.sigmoid(jnp.dot(xb, wa_ref[...], preferred_element_type=F32) + ba_ref[...])
    i = jax.nn.sigmoid(jnp.dot(xb, wx_ref[...], preferred_element_type=F32) + bx_ref[...])
    log_a = -LRU_C * r * _softplus(-lam_ref[...])
    a = jnp.exp(log_a)
    u = jnp.sqrt(-_expm1(2.0 * log_a)) * (i * x2)

    a3 = a.reshape(nb, t, c)
    b3 = u.reshape(nb, t, c)
    row = lax.broadcasted_iota(jnp.int32, (nb, t, c), 1)
    step = 1
    while step < t:
        a_prev = jnp.where(row >= step, pltpu.roll(a3, step, axis=1), 1.0)
        b_prev = jnp.where(row >= step, pltpu.roll(b3, step, axis=1), 0.0)
        b3 = a3 * b_prev + b3
        a3 = a3 * a_prev
        step *= 2
    h = a3 * ch_scr[...] + b3

    ch_scr[...] = h[:, t - 1:t, :]
    tail = x[:, t - SUBLANES:, :]
    cx_scr[...] = tail
    conv_ref[...] = tail
    hlast_ref[...] = h[:, t - SUBLANES:, :]

    rnn = h * _gelu(gr_ref[...])
    ms = jnp.mean(rnn * rnn, axis=-1, keepdims=True)
    rnn_ref[...] = (rnn * lax.rsqrt(ms + EPS) * gain_ref[...]).reshape(nb * t, c)


def _rglru(xr, gr, buf0, h0, conv_w, conv_b, wa_bd, ba, wx_bd, bx, lam, gain, nb, t):
    bt, tt, c = xr.shape
    g0, g1 = bt // nb, tt // t
    const = lambda shape: pl.BlockSpec(shape, lambda i, j: (0,) * len(shape))
    per_b = lambda r: pl.BlockSpec((nb, r, c), lambda i, j: (i, 0, 0))
    return pl.pallas_call(
        _rglru_kernel,
        grid=(g0, g1),
        in_specs=[
            pl.BlockSpec((nb, t, c), lambda i, j: (i, j, 0)),
            pl.BlockSpec((nb, t, c), lambda i, j: (i, j, 0)),
            per_b(SUBLANES),
            per_b(1),
            const((CONV_WIDTH, c)),
            const((1, c)),
            const((c, c)),
            const((1, c)),
            const((c, c)),
            const((1, c)),
            const((1, c)),
            const((1, c)),
        ],
        out_specs=[
            pl.BlockSpec((nb * t, c), lambda i, j: (i * g1 + j, 0)),
            per_b(SUBLANES),
            per_b(SUBLANES),
        ],
        out_shape=[
            jax.ShapeDtypeStruct((bt * tt, c), F32),
            jax.ShapeDtypeStruct((bt, SUBLANES, c), F32),
            jax.ShapeDtypeStruct((bt, SUBLANES, c), F32),
        ],
        scratch_shapes=[pltpu.VMEM((nb, SUBLANES, c), F32), pltpu.VMEM((nb, 1, c), F32)],
        compiler_params=_cparams(2),
        name="rglru",
    )(xr, gr, buf0, h0, conv_w, conv_b, wa_bd, ba, wx_bd, bx, lam, gain)


def _mix_kernel(attn_ref, rnn_ref, x_ref, mod_ref, ag_ref, wout_ref, nff_ref, wq_ref, x1_ref, h2_ref, qp_ref):
    nb, t, d = x_ref.shape
    attn = attn_ref[...]
    ms = jnp.mean(attn * attn, axis=-1, keepdims=True)
    attn_n = attn * lax.rsqrt(ms + EPS) * ag_ref[...]
    mixed = jnp.dot(attn_n.astype(BF16), wout_ref[0:D_ATTN, :], preferred_element_type=F32)
    mixed = mixed + jnp.dot(rnn_ref[...].astype(BF16), wout_ref[D_ATTN:, :], preferred_element_type=F32)
    mod = mod_ref[...]
    x1 = x_ref[...] + mod[:, 2:3, :] * mixed.reshape(nb, t, d)
    ms1 = jnp.mean(x1 * x1, axis=-1, keepdims=True)
    h = x1 * lax.rsqrt(ms1 + EPS) * nff_ref[...]
    h = h * (1.0 + mod[:, 4:5, :]) + mod[:, 3:4, :]
    h2 = h.reshape(nb * t, d).astype(BF16)
    x1_ref[...] = x1
    h2_ref[...] = h2
    qp = jnp.dot(h2, wq_ref[...], preferred_element_type=F32).astype(BF16)
    for hp in range(qp_ref.shape[0]):
        qp_ref[hp] = qp[:, hp * PEER_NKEYS:(hp + 1) * PEER_NKEYS]


def _mix(attn, rnn_n, x, mod, attn_gain, w_out_bf, norm_ff, wq_bf, nb, t):
    bt, tt, d = x.shape
    g0, g1 = bt // nb, tt // t
    rows = nb * t
    n_rows = bt * tt
    d_rnn = rnn_n.shape[1]
    dq = wq_bf.shape[1]
    const = lambda shape: pl.BlockSpec(shape, lambda i, j: (0,) * len(shape))
    row_spec = lambda w: pl.BlockSpec((rows, w), lambda i, j: (i * g1 + j, 0))
    return pl.pallas_call(
        _mix_kernel,
        grid=(g0, g1),
        in_specs=[
            row_spec(D_ATTN),
            row_spec(d_rnn),
            pl.BlockSpec((nb, t, d), lambda i, j: (i, j, 0)),
            pl.BlockSpec((nb, N_MOD, d), lambda i, j: (i, 0, 0)),
            const((1, D_ATTN)),
            const((D_ATTN + d_rnn, d)),
            const((1, d)),
            const((d, dq)),
        ],
        out_specs=[pl.BlockSpec((nb, t, d), lambda i, j: (i, j, 0)), row_spec(d),
                   pl.BlockSpec((dq // PEER_NKEYS, rows, PEER_NKEYS), lambda i, j: (0, i * g1 + j, 0))],
        out_shape=[
            jax.ShapeDtypeStruct((bt, tt, d), F32),
            jax.ShapeDtypeStruct((n_rows, d), BF16),
            jax.ShapeDtypeStruct((dq // PEER_NKEYS, n_rows, PEER_NKEYS), BF16),
        ],
        compiler_params=_cparams(2),
        name="mix",
    )(attn, rnn_n, x, mod, attn_gain, w_out_bf, norm_ff, wq_bf)


def _top_values(work, k):
    vals = []
    for _ in range(k):
        mx = jnp.max(work, axis=0, keepdims=True)
        vals.append(mx)
        work = jnp.where(work == mx, -jnp.inf, work)
    return jnp.concatenate(vals, axis=0)


def _extract_topk(work, k):
    n_rows = work.shape[0]
    rid = lax.broadcasted_iota(jnp.int32, work.shape, 0).astype(F32)
    rank = jnp.full(work.shape, RANK_BIG, F32)
    vals = []
    for step in range(k):
        mx = jnp.max(work, axis=0, keepdims=True)
        idx = jnp.min(jnp.where(work == mx, rid, float(n_rows)), axis=0, keepdims=True)
        hit = rid == idx
        rank = jnp.where(hit, float(step), rank)
        work = jnp.where(hit, -jnp.inf, work)
        vals.append(mx)
    return jnp.concatenate(vals, axis=0), rank


def _staircase_sums(a, b):
    k = a.shape[0]
    row = lax.broadcasted_iota(jnp.int32, (SUBLANES, a.shape[1]), 0)
    pieces = [a[0:1] + b]
    for k0 in range(1, SUBLANES):
        n1 = k // (k0 + 1)
        blk = a[k0:k0 + 1] + b[0:SUBLANES]
        pieces.append(blk if n1 >= SUBLANES else jnp.where(row < n1, blk, -jnp.inf))
    for k0 in range(SUBLANES, k, SUBLANES):
        pieces.append(a[k0:k0 + SUBLANES] + b[0:1])
    return jnp.concatenate(pieces, axis=0)


def _kth_largest(work, k):
    kf = float(k)
    cum = jnp.zeros((1, work.shape[1]), F32)
    tau = cum
    z = cum
    top = None
    for _ in range(k):
        mx = jnp.max(work, axis=0, keepdims=True)
        top = mx if top is None else top
        eq = work == mx
        cnt = jnp.sum(jnp.where(eq, 1.0, 0.0), axis=0, keepdims=True)
        still = cum < kf
        tau = jnp.where(still, mx, tau)
        z = z + jnp.where(still, cnt * jnp.exp(mx - top), 0.0)
        cum = jnp.where(still, cum + cnt, cum)
        work = jnp.where(eq, -jnp.inf, work)
    return tau, cum, z


def _pair_sums(sv0, sv1):
    return jnp.concatenate([sv0[k:k + 1] + sv1 for k in range(sv0.shape[0])], axis=0)


def _peer_kernel(x1_ref, mod_ref, h2_ref, qp_ref, keys_ref, u_ref, v_ref, y_ref,
                 sm_scr, e_scr, r_scr, sv_scr, tau_scr, ptau_scr, st_scr, a_scr, wt_scr, acc_scr, flag_ref):
    et = pl.program_id(2)
    n_et = pl.num_programs(2)
    nb, t, d = x1_ref.shape
    c_tok = nb * t
    n_chunks = c_tok // LANES
    te = u_ref.shape[0]
    i_per_tile = te // PEER_NKEYS
    k_top = float(PEER_TOPK)

    def chunk(c):
        return slice(c * LANES, (c + 1) * LANES)

    def scores(hp):
        st_scr[...] = _dot_nt(keys_ref[hp], qp_ref[hp])

    def select_fast():
        def store_half(hp, c0, s, sv, member):
            sm = jnp.where(member, s, -jnp.inf)
            e = jnp.where(member, jnp.exp(s - sv[0:1]), 0.0)
            for k in range(s.shape[1] // LANES):
                sv_scr[hp, c0 + k] = sv[:, chunk(k)]
                sm_scr[hp, c0 + k] = sm[:, chunk(k)]
                e_scr[hp, c0 + k] = e[:, chunk(k)]

        def half_body(hp, bad):
            scores(hp)
            for w in range(c_tok // PEER_SELECT_LANES):
                c0 = w * (PEER_SELECT_LANES // LANES)
                s = st_scr[:, w * PEER_SELECT_LANES:(w + 1) * PEER_SELECT_LANES]
                sv = _top_values(s, PEER_TOPK)
                member = s >= sv[PEER_TOPK - 1:PEER_TOPK]
                n_mem = jnp.sum(jnp.where(member, 1.0, 0.0), axis=0, keepdims=True)
                store_half(hp, c0, s, sv, member)

                @pl.when(jnp.max(jnp.where(n_mem != k_top, 1.0, 0.0)) > 0.0)
                def _():
                    sv_x, rank = _extract_topk(s, PEER_TOPK)
                    store_half(hp, c0, s, sv_x, rank < k_top)
            return bad

        bad = lax.fori_loop(0, 2 * PEER_HEADS, half_body, jnp.zeros((1, LANES), F32))

        def head_body(h, bad):
            for c in range(n_chunks):
                cand = _staircase_sums(sv_scr[2 * h, c], sv_scr[2 * h + 1, c])
                tau, n_ge, z = _kth_largest(cand, PEER_TOPK)
                bad = jnp.maximum(bad, jnp.where(n_ge != k_top, 1.0, 0.0))
                tau_scr[h, c] = tau
                e_scr[2 * h, c] = e_scr[2 * h, c] / z
                sv0, sv1, s0m = sv_scr[2 * h, c], sv_scr[2 * h + 1, c], sm_scr[2 * h, c]
                theta = jnp.full(s0m.shape, jnp.inf, F32)
                for k0 in range(PEER_TOPK):
                    reach = jnp.where(sv0[k0:k0 + 1] + sv1 >= tau, sv1, jnp.inf)
                    theta = jnp.where(s0m == sv0[k0:k0 + 1], jnp.min(reach, axis=0, keepdims=True), theta)
                sm_scr[2 * h, c] = theta
            return bad

        return lax.fori_loop(0, PEER_HEADS, head_body, bad)

    def select_exact():
        def half_body(hp, carry):
            scores(hp)
            for c in range(n_chunks):
                s = st_scr[:, chunk(c)]
                sv, rank = _extract_topk(s, PEER_TOPK)
                member = rank < k_top
                sv_scr[hp, c] = sv
                sm_scr[hp, c] = jnp.where(member, s, -jnp.inf)
                e_scr[hp, c] = jnp.where(member, jnp.exp(s - sv[0:1]), 0.0)
                r_scr[hp, c] = rank
            return carry

        lax.fori_loop(0, 2 * PEER_HEADS, half_body, 0)

        def head_body(h, carry):
            for c in range(n_chunks):
                cand = _pair_sums(sv_scr[2 * h, c], sv_scr[2 * h + 1, c])
                fv, crank = _extract_topk(cand, PEER_TOPK)
                z = jnp.sum(jnp.exp(fv - fv[0:1]), axis=0, keepdims=True)
                pos = lax.broadcasted_iota(jnp.int32, cand.shape, 0).astype(F32)
                tau_scr[h, c] = fv[PEER_TOPK - 1:PEER_TOPK]
                ptau_scr[h, c] = jnp.sum(jnp.where(crank == k_top - 1.0, pos, 0.0), axis=0, keepdims=True)
                e_scr[2 * h, c] = e_scr[2 * h, c] / z
                r_scr[2 * h, c] = r_scr[2 * h, c] * k_top
            return carry

        lax.fori_loop(0, PEER_HEADS, head_body, 0)

    @pl.when(et == 0)
    def _():
        bad = select_fast()
        flag_ref[0] = jnp.max(bad).astype(jnp.int32)
        acc_scr[...] = jnp.zeros(acc_scr.shape, F32)

    @pl.when((et == 0) & (flag_ref[0] != 0))
    def _():
        select_exact()

    a_scr[...] = _dot_nt(u_ref[...], h2_ref[...])

    def weights(exact_ties):
        def body(ii, carry):
            i = et * i_per_tile + ii
            rows = pl.ds(pl.multiple_of(ii * PEER_NKEYS, PEER_NKEYS), PEER_NKEYS)
            for c in range(n_chunks):
                g = None
                for h in range(PEER_HEADS):
                    val = e_scr[2 * h + 1, c] * e_scr[2 * h, c, pl.ds(i, 1), :]
                    if exact_ties:
                        tsum = sm_scr[2 * h + 1, c] + sm_scr[2 * h, c, pl.ds(i, 1), :]
                        tau = tau_scr[h, c]
                        pos = r_scr[2 * h + 1, c] + r_scr[2 * h, c, pl.ds(i, 1), :]
                        keep = (tsum > tau) | ((tsum == tau) & (pos <= ptau_scr[h, c]))
                    else:
                        keep = sm_scr[2 * h + 1, c] >= sm_scr[2 * h, c, pl.ds(i, 1), :]
                    contrib = jnp.where(keep, val, 0.0)
                    g = contrib if g is None else g + contrib
                wt_scr[rows, chunk(c)] = (g * _gelu_tanh(a_scr[rows, chunk(c)])).astype(BF16)
            return carry

        lax.fori_loop(0, i_per_tile, body, 0)

    @pl.when(flag_ref[0] == 0)
    def _():
        weights(False)

    @pl.when(flag_ref[0] != 0)
    def _():
        weights(True)

    acc_scr[...] += jnp.dot(v_ref[...], wt_scr[...], preferred_element_type=F32)

    @pl.when(et == n_et - 1)
    def _():
        y_ref[...] = x1_ref[...] + mod_ref[...][:, 5:6, :] * acc_scr[...].T.reshape(nb, t, d)


def _peer(x1, mod, h2, qp, keys_bf, u_bf, vt_bf, nb, t):
    bt, tt, d = x1.shape
    g0, g1 = bt // nb, tt // t
    c_tok = nb * t
    n_chunks = c_tok // LANES
    n_exp = u_bf.shape[0]
    te = PEER_EXPERT_TILE
    n_et = n_exp // te
    n_hp = qp.shape[0]
    tok3 = pl.BlockSpec((nb, t, d), lambda i, j, e: (i, j, 0))
    halves = lambda r: pltpu.VMEM((n_hp, n_chunks, r, LANES), F32)
    heads = lambda r: pltpu.VMEM((PEER_HEADS, n_chunks, r, LANES), F32)
    return pl.pallas_call(
        _peer_kernel,
        grid=(g0, g1, n_et),
        in_specs=[
            tok3,
            pl.BlockSpec((nb, N_MOD, d), lambda i, j, e: (i, 0, 0)),
            pl.BlockSpec((c_tok, d), lambda i, j, e: (i * g1 + j, 0)),
            pl.BlockSpec((n_hp, c_tok, PEER_NKEYS), lambda i, j, e: (0, i * g1 + j, 0)),
            pl.BlockSpec(keys_bf.shape, lambda i, j, e: (0, 0, 0)),
            pl.BlockSpec((te, d), lambda i, j, e: (e, 0)),
            pl.BlockSpec((d, te), lambda i, j, e: (0, e)),
        ],
        out_specs=tok3,
        out_shape=jax.ShapeDtypeStruct((bt, tt, d), F32),
        scratch_shapes=[
            halves(PEER_NKEYS), halves(PEER_NKEYS), halves(PEER_NKEYS), halves(PEER_TOPK),
            heads(1), heads(1),
            pltpu.VMEM((PEER_NKEYS, c_tok), F32),
            pltpu.VMEM((te, c_tok), F32),
            pltpu.VMEM((te, c_tok), BF16),
            pltpu.VMEM((d, c_tok), F32),
            pltpu.SMEM((1,), jnp.int32),
        ],
        compiler_params=_cparams(3),
        name="peer",
    )(x1, mod, h2, qp, keys_bf, u_bf, vt_bf)


def _block_diag(w):
    n, c, d = w.shape
    eye = jnp.eye(n, dtype=w.dtype)
    return (eye[:, None, :, None] * w[:, :, None, :]).reshape(n * c, n * d)


def kernel(x_prompt, x_sample, cache_k, cache_v, state_conv, state_h, page_table, c_prompt, c_sample, w_ada, b_ada, norm_mix, norm_ff, w_in, q_gain, k_gain, conv_w, conv_b, lru_wa, lru_ba, lru_wx, lru_bx, lru_lambda, attn_out_gain, rnn_out_gain, w_out, peer_wq, peer_keys, peer_u, peer_v):
    depth = w_ada.shape[0]
    batch, seq, d_model = x_prompt.shape
    dec_batch, dec_seq, _ = x_sample.shape
    _, n_pool, page_size, n_heads, head_dim = cache_k.shape
    n_pages = page_table.shape[1]
    d_rnn = d_model - D_ATTN
    assert (n_heads, head_dim) == (N_HEADS, HEAD_DIM)
    assert seq % ROW_TILE == 0 and ROW_TILE % MOBA_BLOCK == 0 and ROW_TILE % dec_seq == 0
    assert dec_seq == SUBLANES and (dec_batch * dec_seq) % ROW_TILE == 0
    assert (n_pages * page_size) % MOBA_BLOCK == 0

    head_mean = _block_diag(jnp.full((N_HEADS, HEAD_DIM, HEAD_DIM), 1.0 / HEAD_DIM, F32)).astype(BF16)
    cache_k_pages = cache_k.transpose(0, 1, 3, 4, 2).reshape(depth * n_pool, D_ATTN, page_size)
    cache_v_pages = cache_v.transpose(0, 1, 3, 4, 2).reshape(depth * n_pool, D_ATTN, page_size)
    page_table_flat = page_table.reshape(-1).astype(jnp.int32)
    nb_s = ROW_TILE // dec_seq

    yp, ys = x_prompt, x_sample
    outs = [[] for _ in range(8)]
    for l in range(depth):
        mod = _ada(jnp.concatenate([c_prompt, c_sample], axis=0), w_ada[l], b_ada[l])
        mod = mod.reshape(batch + dec_batch, N_MOD, d_model)
        mod_p, mod_s = mod[:batch], mod[batch:]
        w_in_bf = w_in[l].astype(BF16)
        w_out_bf = w_out[l].astype(BF16)
        wq_bf = peer_wq[l].astype(BF16)
        keys_bf = peer_keys[l].reshape(2 * PEER_HEADS, PEER_NKEYS, -1).astype(BF16)
        u_bf = peer_u[l].astype(BF16)
        vt_bf = peer_v[l].T.astype(BF16)
        wa_bd = _block_diag(lru_wa[l]).astype(BF16)
        wx_bd = _block_diag(lru_wx[l]).astype(BF16)
        row = lambda a: a.reshape(1, -1)
        q_gain_t = row(jnp.tile(q_gain[l], N_HEADS))
        k_gain_t = row(jnp.tile(k_gain[l], N_HEADS))
        rglru_w = (conv_w[l], row(conv_b[l]), wa_bd, row(lru_ba[l]), wx_bd, row(lru_bx[l]),
                   row(lru_lambda[l]), row(rnn_out_gain[l]))

        q, k, v, xr, gr, kbf, vbf, ksum = _proj(yp, mod_p, row(norm_mix[l]), w_in_bf, head_mean, q_gain_t, k_gain_t,
                                                1, ROW_TILE, True)
        attn = _moba_prompt(q, kbf, vbf, ksum.reshape(batch, seq // MOBA_BLOCK, D_ATTN), batch, seq)
        rnn_n, conv_p, h_p = _rglru(xr.reshape(batch, seq, d_rnn), gr.reshape(batch, seq, d_rnn),
                                    jnp.zeros((batch, SUBLANES, d_rnn), F32), jnp.zeros((batch, 1, d_rnn), F32),
                                    *rglru_w, 1, ROW_TILE)
        x1, h2, qp = _mix(attn, rnn_n, yp, mod_p, row(attn_out_gain[l]), w_out_bf, row(norm_ff[l]), wq_bf, 1, ROW_TILE)
        yp = _peer(x1, mod_p, h2, qp, keys_bf, u_bf, vt_bf, 1, ROW_TILE)
        outs[0].append(k.reshape(batch, seq, N_HEADS, HEAD_DIM))
        outs[1].append(v.reshape(batch, seq, N_HEADS, HEAD_DIM))
        outs[2].append(conv_p[:, SUBLANES - (CONV_WIDTH - 1):])
        outs[3].append(h_p[:, SUBLANES - 1])

        q, k, v, xr, gr = _proj(ys, mod_s, row(norm_mix[l]), w_in_bf, head_mean, q_gain_t, k_gain_t,
                                nb_s, dec_seq, False)
        tok3 = lambda a: a.reshape(dec_batch, dec_seq, -1)
        attn = _moba_sample(tok3(q), tok3(k), tok3(v), cache_k_pages, cache_v_pages, page_table_flat,
                            l * n_pool, n_pages, page_size)
        buf0 = jnp.pad(state_conv[l], ((0, 0), (SUBLANES - (CONV_WIDTH - 1), 0), (0, 0)))
        rnn_n, conv_s, h_s = _rglru(tok3(xr), tok3(gr), buf0, state_h[l][:, None, :], *rglru_w, nb_s, dec_seq)
        x1, h2, qp = _mix(attn.reshape(dec_batch * dec_seq, D_ATTN), rnn_n, ys, mod_s, row(attn_out_gain[l]),
                          w_out_bf, row(norm_ff[l]), wq_bf, nb_s, dec_seq)
        ys = _peer(x1, mod_s, h2, qp, keys_bf, u_bf, vt_bf, nb_s, dec_seq)
        outs[4].append(k.reshape(dec_batch, dec_seq, N_HEADS, HEAD_DIM))
        outs[5].append(v.reshape(dec_batch, dec_seq, N_HEADS, HEAD_DIM))
        outs[6].append(conv_s[:, SUBLANES - (CONV_WIDTH - 1):])
        outs[7].append(h_s[:, SUBLANES - 1])
    return (yp, ys) + tuple(jnp.stack(o) for o in outs)
```

```python
import functools

import jax
import jax.numpy as jnp
from jax import lax
from jax.experimental import pallas as pl
from jax.experimental.pallas import tpu as pltpu

F32 = jnp.float32
BF16 = jnp.bfloat16

N_HEADS = 8
HEAD_DIM = 64
D_ATTN = N_HEADS * HEAD_DIM
CONV_WIDTH = 4
LRU_C = 8.0
MOBA_BLOCK = 256
MOBA_TOPK = 3
PEER_HEADS = 8
PEER_NKEYS = 128
PEER_TOPK = 16
N_MOD = 6
EPS = 1e-6

ROW_TILE = 512
PEER_EXPERT_TILE = 1024
PEER_SELECT_LANES = 512
MOBA_SAMPLE_BLOCKS_PER_STEP = 8
SUBLANES = 8
LANES = 128
NEG_BIG = -1e30
RANK_BIG = 1e6
VMEM_LIMIT = 48 * 1024 * 1024


def _cparams(n_axes, vmem=VMEM_LIMIT):
    return pltpu.CompilerParams(dimension_semantics=("arbitrary",) * n_axes, vmem_limit_bytes=vmem)


def _dot_nt(a, b, precision=None):
    return lax.dot_general(a, b, (((1,), (1,)), ((), ())), precision=precision, preferred_element_type=F32)


def _gelu(x):
    return jax.nn.gelu(x)


def _gelu_tanh(x):
    c = 0.7978845608028654
    half = 0.5 * x
    return half + half * jnp.tanh(x * (c + (c * 0.044715) * (x * x)))


def _ada_kernel(c_ref, w_ref, b_ref, o_ref):
    c = c_ref[...]
    s = c * jax.nn.sigmoid(c)
    o_ref[...] = jnp.dot(s.astype(BF16), w_ref[...].astype(BF16), preferred_element_type=F32) + b_ref[...]


def _ada(c_all, w_ada, b_ada):
    n, d = c_all.shape
    n_out = w_ada.shape[1]
    tile = d
    return pl.pallas_call(
        _ada_kernel,
        grid=(n_out // tile,),
        in_specs=[
            pl.BlockSpec((n, d), lambda j: (0, 0)),
            pl.BlockSpec((d, tile), lambda j: (0, j)),
            pl.BlockSpec((1, tile), lambda j: (0, j)),
        ],
        out_specs=pl.BlockSpec((n, tile), lambda j: (0, j)),
        out_shape=jax.ShapeDtypeStruct((n, n_out), F32),
        compiler_params=_cparams(1),
        name="ada",
    )(c_all, w_ada, b_ada.reshape(1, n_out))


def _proj_kernel(x_ref, mod_ref, nw_ref, win_ref, hm_ref, qg_ref, kg_ref, *out_refs, attn_extras):
    nb, t, d = x_ref.shape
    x = x_ref[...]
    mod = mod_ref[...]
    ms = jnp.mean(x * x, axis=-1, keepdims=True)
    h = x * lax.rsqrt(ms + EPS) * nw_ref[...]
    h = h * (1.0 + mod[:, 1:2, :]) + mod[:, 0:1, :]
    proj = jnp.dot(h.reshape(nb * t, d).astype(BF16), win_ref[...], preferred_element_type=F32)
    q = proj[:, 0:D_ATTN]
    k = proj[:, D_ATTN:2 * D_ATTN]
    v = proj[:, 2 * D_ATTN:3 * D_ATTN]
    d_rnn = (proj.shape[1] - 3 * D_ATTN) // 2
    xr = proj[:, 3 * D_ATTN:3 * D_ATTN + d_rnn]
    gr = proj[:, 3 * D_ATTN + d_rnn:]
    hm = hm_ref[...]

    def head_norm(a, gain):
        msq = jnp.dot((a * a).astype(BF16), hm, preferred_element_type=F32)
        return a * lax.rsqrt(msq + EPS) * gain

    q = head_norm(q, qg_ref[...])
    k = head_norm(k, kg_ref[...])
    q_ref, k_ref, v_ref, xr_ref, gr_ref = out_refs[:5]
    q_ref[...] = q
    k_ref[...] = k
    v_ref[...] = v
    xr_ref[...] = xr
    gr_ref[...] = gr
    if attn_extras:
        kbf_ref, vbf_ref, ksum_ref = out_refs[5:]
        kbf_ref[...] = k.astype(BF16)
        vbf_ref[...] = v.astype(BF16)
        n_blk = (nb * t) // MOBA_BLOCK
        sums = [jnp.sum(k[i * MOBA_BLOCK:(i + 1) * MOBA_BLOCK], axis=0, keepdims=True) for i in range(n_blk)]
        ksum_ref[0] = jnp.concatenate(sums, axis=0)


def _proj(x, mod, norm_w, w_in_bf, head_mean, q_gain_t, k_gain_t, nb, t, attn_extras):
    bt, tt, d = x.shape
    g0, g1 = bt // nb, tt // t
    rows = nb * t
    n_rows = bt * tt
    d_in = w_in_bf.shape[1]
    d_rnn = (d_in - 3 * D_ATTN) // 2
    row_spec = lambda w: pl.BlockSpec((rows, w), lambda i, j: (i * g1 + j, 0))
    const = lambda shape: pl.BlockSpec(shape, lambda i, j: (0,) * len(shape))
    out_specs = [row_spec(D_ATTN)] * 3 + [row_spec(d_rnn)] * 2
    out_shape = [jax.ShapeDtypeStruct((n_rows, D_ATTN), F32)] * 3 + [jax.ShapeDtypeStruct((n_rows, d_rnn), F32)] * 2
    if attn_extras:
        n_blk = rows // MOBA_BLOCK
        out_specs += [row_spec(D_ATTN)] * 2 + [pl.BlockSpec((1, n_blk, D_ATTN), lambda i, j: (i * g1 + j, 0, 0))]
        out_shape += [jax.ShapeDtypeStruct((n_rows, D_ATTN), BF16)] * 2
        out_shape += [jax.ShapeDtypeStruct((g0 * g1, n_blk, D_ATTN), F32)]
    return pl.pallas_call(
        functools.partial(_proj_kernel, attn_extras=attn_extras),
        grid=(g0, g1),
        in_specs=[
            pl.BlockSpec((nb, t, d), lambda i, j: (i, j, 0)),
            pl.BlockSpec((nb, N_MOD, d), lambda i, j: (i, 0, 0)),
            const((1, d)),
            const((d, d_in)),
            const((D_ATTN, D_ATTN)),
            const((1, D_ATTN)),
            const((1, D_ATTN)),
        ],
        out_specs=out_specs,
        out_shape=out_shape,
        compiler_params=_cparams(2),
        name="proj",
    )(x, mod, norm_w, w_in_bf, head_mean, q_gain_t, k_gain_t)


def _moba_prompt_kernel(q_ref, k_ref, v_ref, ksum_ref, o_ref):
    qb = pl.program_id(1)
    tq = q_ref.shape[0]
    s_len = k_ref.shape[0]
    n_blk = s_len // MOBA_BLOCK
    q = q_ref[...]
    kall = k_ref[...]
    vall = v_ref[...]
    kmean = ksum_ref[0] * (1.0 / MOBA_BLOCK)
    row = lax.broadcasted_iota(jnp.int32, (tq, s_len), 0)
    col = lax.broadcasted_iota(jnp.int32, (tq, s_len), 1)
    t_pos = qb * tq + row
    dist = (t_pos - col).astype(F32)
    own = ((col // MOBA_BLOCK) == qb) & (col <= t_pos)
    blk_of_col = lax.broadcasted_iota(jnp.int32, (n_blk, s_len), 1) // MOBA_BLOCK
    expand = (blk_of_col == lax.broadcasted_iota(jnp.int32, (n_blk, s_len), 0)).astype(BF16)
    n_iota = lax.broadcasted_iota(jnp.int32, (tq, n_blk), 1)
    outs = []
    for h in range(N_HEADS):
        hs = slice(h * HEAD_DIM, (h + 1) * HEAD_DIM)
        qh = q[:, hs]
        gate = _dot_nt(qh, kmean[:, hs], precision=lax.Precision.HIGHEST)
        cnt = jnp.zeros((tq, n_blk), jnp.int32)
        for m in range(n_blk):
            gm = gate[:, m:m + 1]
            beats = (gm > gate) | ((gm == gate) & (m < n_iota))
            cnt = cnt + jnp.where(beats, (m < qb).astype(jnp.int32), 0)
        sel = ((cnt < MOBA_TOPK) & (n_iota < qb)).astype(BF16)
        sel_cols = jnp.dot(sel, expand, preferred_element_type=F32)
        mask = own | (sel_cols > 0.5)
        slope = 2.0 ** (-8.0 * (h + 1) / N_HEADS)
        s = _dot_nt(qh.astype(BF16), kall[:, hs]) * (HEAD_DIM ** -0.5) - slope * dist
        s = jnp.where(mask, s, NEG_BIG)
        mx = jnp.max(s, axis=-1, keepdims=True)
        p = jnp.exp(s - mx)
        l = jnp.sum(p, axis=-1, keepdims=True)
        o = jnp.dot(p.astype(BF16), vall[:, hs], preferred_element_type=F32)
        outs.append(o / l)
    o_ref[...] = jnp.concatenate(outs, axis=1)


def _moba_prompt(q, kbf, vbf, ksum, batch, seq):
    nq = seq // MOBA_BLOCK
    return pl.pallas_call(
        _moba_prompt_kernel,
        grid=(batch, nq),
        in_specs=[
            pl.BlockSpec((MOBA_BLOCK, D_ATTN), lambda b, i: (b * nq + i, 0)),
            pl.BlockSpec((seq, D_ATTN), lambda b, i: (b, 0)),
            pl.BlockSpec((seq, D_ATTN), lambda b, i: (b, 0)),
            pl.BlockSpec((1, nq, D_ATTN), lambda b, i: (b, 0, 0)),
        ],
        out_specs=pl.BlockSpec((MOBA_BLOCK, D_ATTN), lambda b, i: (b * nq + i, 0)),
        out_shape=jax.ShapeDtypeStruct((batch * seq, D_ATTN), F32),
        compiler_params=_cparams(2),
        name="moba_prompt",
    )(q, kbf, vbf, ksum)


def _moba_sample_kernel(pt_ref, q_ref, kn_ref, vn_ref, *refs, past_len, blocks_per_step, pages_per_block):
    del pt_ref
    n_pg = blocks_per_step * pages_per_block
    k_refs, v_refs = refs[:n_pg], refs[n_pg:2 * n_pg]
    o_ref, qhi_scr, qlo_scr, gate_scr, m_scr, l_scr, o_scr = refs[2 * n_pg:]
    step = pl.program_id(1)
    n_steps = pl.num_programs(1)
    t_new = q_ref.shape[1]
    rows = N_HEADS * t_new
    r_iota = lax.broadcasted_iota(jnp.int32, (rows, 1), 0)
    t_of_r = r_iota % t_new
    h_of_r = r_iota // t_new
    slope = jnp.exp2(-8.0 * (h_of_r + 1).astype(F32) / N_HEADS)
    lane = lax.broadcasted_iota(jnp.int32, (rows, LANES), 1)
    scale = HEAD_DIM ** -0.5

    @pl.when(step == 0)
    def _():
        qt = jnp.concatenate([q_ref[0]] * N_HEADS, axis=0)
        rr = lax.broadcasted_iota(jnp.int32, (rows, D_ATTN), 0) // t_new
        cc = lax.broadcasted_iota(jnp.int32, (rows, D_ATTN), 1) // HEAD_DIM
        qbd = jnp.where(rr == cc, qt, 0.0)
        q_hi = qbd.astype(BF16)
        qhi_scr[...] = q_hi
        qlo_scr[...] = (qbd - q_hi.astype(F32)).astype(BF16)
        gate_scr[...] = jnp.full((rows, LANES), -jnp.inf, F32)
        m_scr[...] = jnp.zeros((rows, LANES), F32)
        l_scr[...] = jnp.zeros((rows, LANES), F32)

    qbd_bf = qhi_scr[...]
    qbd_lo = qlo_scr[...]
    gate_all, m_all, l_all = gate_scr[...], m_scr[...], l_scr[...]
    for j in range(blocks_per_step):
        n = step * blocks_per_step + j
        pages = slice(j * pages_per_block, (j + 1) * pages_per_block)
        kt = jnp.concatenate([r[0] for r in k_refs[pages]], axis=1)
        vt = jnp.concatenate([r[0] for r in v_refs[pages]], axis=1)
        kmean = jnp.sum(kt, axis=1, keepdims=True) * (1.0 / MOBA_BLOCK)
        km_hi = kmean.astype(BF16)
        km_lo = (kmean - km_hi.astype(F32)).astype(BF16)
        gate_n = (jnp.dot(qbd_bf, km_hi, preferred_element_type=F32) + jnp.dot(qbd_bf, km_lo, preferred_element_type=F32)
                  + jnp.dot(qbd_lo, km_hi, preferred_element_type=F32))
        s = jnp.dot(qbd_bf, kt.astype(BF16), preferred_element_type=F32) * scale
        pos = n * MOBA_BLOCK + lax.broadcasted_iota(jnp.int32, (rows, MOBA_BLOCK), 1)
        s = s - slope * ((past_len + t_of_r) - pos).astype(F32)
        m_n = jnp.max(s, axis=1, keepdims=True)
        p = jnp.exp(s - m_n)
        l_n = jnp.sum(p, axis=1, keepdims=True)
        o_scr[n] = _dot_nt(p.astype(BF16), vt.astype(BF16))
        here = lane == n
        gate_all = jnp.where(here, gate_n, gate_all)
        m_all = jnp.where(here, m_n, m_all)
        l_all = jnp.where(here, l_n, l_all)
    gate_scr[...] = gate_all
    m_scr[...] = m_all
    l_scr[...] = l_all

    @pl.when(step == n_steps - 1)
    def _():
        work = gate_all
        sel = jnp.zeros((rows, LANES), jnp.bool_)
        for _k in range(MOBA_TOPK):
            mx = jnp.max(work, axis=1, keepdims=True)
            idx = jnp.min(jnp.where(work == mx, lane, LANES), axis=1, keepdims=True)
            hit = lane == idx
            sel = sel | hit
            work = jnp.where(hit, -jnp.inf, work)
        kn = kn_ref[0]
        vn = vn_ref[0]
        sn = _dot_nt(qbd_bf, kn.astype(BF16)) * scale
        dn = t_of_r - lax.broadcasted_iota(jnp.int32, (rows, t_new), 1)
        sn = jnp.where(dn >= 0, sn - slope * dn.astype(F32), NEG_BIG)
        big_m = jnp.maximum(jnp.max(jnp.where(sel, m_all, NEG_BIG), axis=1, keepdims=True),
                            jnp.max(sn, axis=1, keepdims=True))
        w = jnp.where(sel, jnp.exp(jnp.where(sel, m_all, big_m) - big_m), 0.0)
        pn = jnp.exp(sn - big_m)
        l_tot = jnp.sum(w * l_all, axis=1, keepdims=True) + jnp.sum(pn, axis=1, keepdims=True)
        acc = jnp.dot(pn.astype(BF16), vn.astype(BF16), preferred_element_type=F32)
        for blk in range(o_scr.shape[0]):
            acc = acc + w[:, blk:blk + 1] * o_scr[blk]
        acc = acc / l_tot
        head_of_lane = lax.broadcasted_iota(jnp.int32, (t_new, D_ATTN), 1) // HEAD_DIM
        out = jnp.zeros((t_new, D_ATTN), F32)
        for h in range(N_HEADS):
            out = out + jnp.where(head_of_lane == h, acc[h * t_new:(h + 1) * t_new], 0.0)
        o_ref[0] = out


def _moba_sample(q, k_new, v_new, cache_kt_pages, cache_vt_pages, page_table_flat, page_base, n_pages, page_size):
    dec_batch, t_new, _ = q.shape
    pages_per_block = MOBA_BLOCK // page_size
    n_full = n_pages // pages_per_block
    blocks_per_step = MOBA_SAMPLE_BLOCKS_PER_STEP
    assert n_full * pages_per_block == n_pages and MOBA_TOPK <= n_full <= LANES and n_full % blocks_per_step == 0
    pages_per_step = blocks_per_step * pages_per_block
    rows = N_HEADS * t_new
    tok_spec = pl.BlockSpec((1, t_new, D_ATTN), lambda b, n, pt: (b, 0, 0))

    def page_spec(which):
        return pl.BlockSpec((1, D_ATTN, page_size),
                            lambda b, n, pt: (page_base + pt[b * n_pages + pages_per_step * n + which], 0, 0))

    page_specs = [page_spec(w) for w in range(pages_per_step)]
    grid_spec = pltpu.PrefetchScalarGridSpec(
        num_scalar_prefetch=1,
        grid=(dec_batch, n_full // blocks_per_step),
        in_specs=[tok_spec, tok_spec, tok_spec] + page_specs + page_specs,
        out_specs=pl.BlockSpec((1, t_new, D_ATTN), lambda b, n, pt: (b, 0, 0)),
        scratch_shapes=[
            pltpu.VMEM((rows, D_ATTN), BF16),
            pltpu.VMEM((rows, D_ATTN), BF16),
            pltpu.VMEM((rows, LANES), F32),
            pltpu.VMEM((rows, LANES), F32),
            pltpu.VMEM((rows, LANES), F32),
            pltpu.VMEM((n_full, rows, D_ATTN), F32),
        ],
    )
    return pl.pallas_call(
        functools.partial(_moba_sample_kernel, past_len=n_pages * page_size, blocks_per_step=blocks_per_step,
                          pages_per_block=pages_per_block),
        grid_spec=grid_spec,
        out_shape=jax.ShapeDtypeStruct((dec_batch, t_new, D_ATTN), F32),
        compiler_params=_cparams(2),
        name="moba_sample",
    )(page_table_flat, q, k_new, v_new, *([cache_kt_pages] * pages_per_step), *([cache_vt_pages] * pages_per_step))


def _softplus(x):
    return jnp.maximum(x, 0.0) + jnp.log1p(jnp.exp(-jnp.abs(x)))


def _expm1(x):
    return jnp.tanh(0.5 * x) * (jnp.exp(x) + 1.0)


def _rglru_kernel(xr_ref, gr_ref, buf0_ref, h0_ref, cw_ref, cb_ref, wa_ref, ba_ref, wx_ref, bx_ref,
                  lam_ref, gain_ref, rnn_ref, conv_ref, hlast_ref, cx_scr, ch_scr):
    j = pl.program_id(1)
    nb, t, c = xr_ref.shape

    @pl.when(j == 0)
    def _():
        cx_scr[...] = buf0_ref[...]
        ch_scr[...] = h0_ref[...]

    x = xr_ref[...]
    ext = jnp.concatenate([cx_scr[...], x], axis=1)
    cw = cw_ref[...]
    xc = cb_ref[...]
    for tap in range(CONV_WIDTH - 1):
        shift = CONV_WIDTH - 1 - tap
        xc = xc + pltpu.roll(ext, shift, axis=1)[:, SUBLANES:, :] * cw[tap:tap + 1]
    xc = xc + x * cw[CONV_WIDTH - 1:CONV_WIDTH]

    x2 = xc.reshape(nb * t, c)
    xb = x2.astype(BF16)
    r = jax.nn.sigmoid(jnp.dot(xb, wa_ref[...], preferred_element_type=F32) + ba_ref[...])
    i = jax.nn.sigmoid(jnp.dot(xb, wx_ref[...], preferred_element_type=F32) + bx_ref[...])
    log_a = -LRU_C * r * _softplus(-lam_ref[...])
    a = jnp.exp(log_a)
    u = jnp.sqrt(-_expm1(2.0 * log_a)) * (i * x2)

    a3 = a.reshape(nb, t, c)
    b3 = u.reshape(nb, t, c)
    row = lax.broadcasted_iota(jnp.int32, (nb, t, c), 1)
    step = 1
    while step < t:
        a_prev = jnp.where(row >= step, pltpu.roll(a3, step, axis=1), 1.0)
        b_prev = jnp.where(row >= step, pltpu.roll(b3, step, axis=1), 0.0)
        b3 = a3 * b_prev + b3
        a3 = a3 * a_prev
        step *= 2
    h = a3 * ch_scr[...] + b3

    ch_scr[...] = h[:, t - 1:t, :]
    tail = x[:, t - SUBLANES:, :]
    cx_scr[...] = tail
    conv_ref[...] = tail
    hlast_ref[...] = h[:, t - SUBLANES:, :]

    rnn = h * _gelu(gr_ref[...])
    ms = jnp.mean(rnn * rnn, axis=-1, keepdims=True)
    rnn_ref[...] = (rnn * lax.rsqrt(ms + EPS) * gain_ref[...]).reshape(nb * t, c)


def _rglru(xr, gr, buf0, h0, conv_w, conv_b, wa_bd, ba, wx_bd, bx, lam, gain, nb, t):
    bt, tt, c = xr.shape
    g0, g1 = bt // nb, tt // t
    const = lambda shape: pl.BlockSpec(shape, lambda i, j: (0,) * len(shape))
    per_b = lambda r: pl.BlockSpec((nb, r, c), lambda i, j: (i, 0, 0))
    return pl.pallas_call(
        _rglru_kernel,
        grid=(g0, g1),
        in_specs=[
            pl.BlockSpec((nb, t, c), lambda i, j: (i, j, 0)),
            pl.BlockSpec((nb, t, c), lambda i, j: (i, j, 0)),
            per_b(SUBLANES),
            per_b(1),
            const((CONV_WIDTH, c)),
            const((1, c)),
            const((c, c)),
            const((1, c)),
            const((c, c)),
            const((1, c)),
            const((1, c)),
            const((1, c)),
        ],
        out_specs=[
            pl.BlockSpec((nb * t, c), lambda i, j: (i * g1 + j, 0)),
            per_b(SUBLANES),
            per_b(SUBLANES),
        ],
        out_shape=[
            jax.ShapeDtypeStruct((bt * tt, c), F32),
            jax.ShapeDtypeStruct((bt, SUBLANES, c), F32),
            jax.ShapeDtypeStruct((bt, SUBLANES, c), F32),
        ],
        scratch_shapes=[pltpu.VMEM((nb, SUBLANES, c), F32), pltpu.VMEM((nb, 1, c), F32)],
        compiler_params=_cparams(2),
        name="rglru",
    )(xr, gr, buf0, h0, conv_w, conv_b, wa_bd, ba, wx_bd, bx, lam, gain)


def _mix_kernel(attn_ref, rnn_ref, x_ref, mod_ref, ag_ref, wout_ref, nff_ref, wq_ref, x1_ref, h2_ref, qp_ref):
    nb, t, d = x_ref.shape
    attn = attn_ref[...]
    ms = jnp.mean(attn * attn, axis=-1, keepdims=True)
    attn_n = attn * lax.rsqrt(ms + EPS) * ag_ref[...]
    mixed = jnp.dot(attn_n.astype(BF16), wout_ref[0:D_ATTN, :], preferred_element_type=F32)
    mixed = mixed + jnp.dot(rnn_ref[...].astype(BF16), wout_ref[D_ATTN:, :], preferred_element_type=F32)
    mod = mod_ref[...]
    x1 = x_ref[...] + mod[:, 2:3, :] * mixed.reshape(nb, t, d)
    ms1 = jnp.mean(x1 * x1, axis=-1, keepdims=True)
    h = x1 * lax.rsqrt(ms1 + EPS) * nff_ref[...]
    h = h * (1.0 + mod[:, 4:5, :]) + mod[:, 3:4, :]
    h2 = h.reshape(nb * t, d).astype(BF16)
    x1_ref[...] = x1
    h2_ref[...] = h2
    qp = jnp.dot(h2, wq_ref[...], preferred_element_type=F32).astype(BF16)
    for hp in range(qp_ref.shape[0]):
        qp_ref[hp] = qp[:, hp * PEER_NKEYS:(hp + 1) * PEER_NKEYS]


def _mix(attn, rnn_n, x, mod, attn_gain, w_out_bf, norm_ff, wq_bf, nb, t):
    bt, tt, d = x.shape
    g0, g1 = bt // nb, tt // t
    rows = nb * t
    n_rows = bt * tt
    d_rnn = rnn_n.shape[1]
    dq = wq_bf.shape[1]
    const = lambda shape: pl.BlockSpec(shape, lambda i, j: (0,) * len(shape))
    row_spec = lambda w: pl.BlockSpec((rows, w), lambda i, j: (i * g1 + j, 0))
    return pl.pallas_call(
        _mix_kernel,
        grid=(g0, g1),
        in_specs=[
            row_spec(D_ATTN),
            row_spec(d_rnn),
            pl.BlockSpec((nb, t, d), lambda i, j: (i, j, 0)),
            pl.BlockSpec((nb, N_MOD, d), lambda i, j: (i, 0, 0)),
            const((1, D_ATTN)),
            const((D_ATTN + d_rnn, d)),
            const((1, d)),
            const((d, dq)),
        ],
        out_specs=[pl.BlockSpec((nb, t, d), lambda i, j: (i, j, 0)), row_spec(d),
                   pl.BlockSpec((dq // PEER_NKEYS, rows, PEER_NKEYS), lambda i, j: (0, i * g1 + j, 0))],
        out_shape=[
            jax.ShapeDtypeStruct((bt, tt, d), F32),
            jax.ShapeDtypeStruct((n_rows, d), BF16),
            jax.ShapeDtypeStruct((dq // PEER_NKEYS, n_rows, PEER_NKEYS), BF16),
        ],
        compiler_params=_cparams(2),
        name="mix",
    )(attn, rnn_n, x, mod, attn_gain, w_out_bf, norm_ff, wq_bf)


def _top_values(work, k):
    vals = []
    for _ in range(k):
        mx = jnp.max(work, axis=0, keepdims=True)
        vals.append(mx)
        work = jnp.where(work == mx, -jnp.inf, work)
    return jnp.concatenate(vals, axis=0)


def _extract_topk(work, k):
    n_rows = work.shape[0]
    rid = lax.broadcasted_iota(jnp.int32, work.shape, 0).astype(F32)
    rank = jnp.full(work.shape, RANK_BIG, F32)
    vals = []
    for step in range(k):
        mx = jnp.max(work, axis=0, keepdims=True)
        idx = jnp.min(jnp.where(work == mx, rid, float(n_rows)), axis=0, keepdims=True)
        hit = rid == idx
        rank = jnp.where(hit, float(step), rank)
        work = jnp.where(hit, -jnp.inf, work)
        vals.append(mx)
    return jnp.concatenate(vals, axis=0), rank


def _staircase_sums(a, b):
    k = a.shape[0]
    row = lax.broadcasted_iota(jnp.int32, (SUBLANES, a.shape[1]), 0)
    pieces = [a[0:1] + b]
    for k0 in range(1, SUBLANES):
        n1 = k // (k0 + 1)
        blk = a[k0:k0 + 1] + b[0:SUBLANES]
        pieces.append(blk if n1 >= SUBLANES else jnp.where(row < n1, blk, -jnp.inf))
    for k0 in range(SUBLANES, k, SUBLANES):
        pieces.append(a[k0:k0 + SUBLANES] + b[0:1])
    return jnp.concatenate(pieces, axis=0)


def _kth_largest(work, k):
    kf = float(k)
    cum = jnp.zeros((1, work.shape[1]), F32)
    tau = cum
    z = cum
    top = None
    for _ in range(k):
        mx = jnp.max(work, axis=0, keepdims=True)
        top = mx if top is None else top
        eq = work == mx
        cnt = jnp.sum(jnp.where(eq, 1.0, 0.0), axis=0, keepdims=True)
        still = cum < kf
        tau = jnp.where(still, mx, tau)
        z = z + jnp.where(still, cnt * jnp.exp(mx - top), 0.0)
        cum = jnp.where(still, cum + cnt, cum)
        work = jnp.where(eq, -jnp.inf, work)
    return tau, cum, z


def _pair_sums(sv0, sv1):
    return jnp.concatenate([sv0[k:k + 1] + sv1 for k in range(sv0.shape[0])], axis=0)


def _peer_kernel(x1_ref, mod_ref, h2_ref, qp_ref, keys_ref, u_ref, v_ref, y_ref,
                 sm_scr, e_scr, r_scr, sv_scr, tau_scr, ptau_scr, st_scr, a_scr, wt_scr, acc_scr, flag_ref):
    et = pl.program_id(2)
    n_et = pl.num_programs(2)
    nb, t, d = x1_ref.shape
    c_tok = nb * t
    n_chunks = c_tok // LANES
    te = u_ref.shape[0]
    i_per_tile = te // PEER_NKEYS
    k_top = float(PEER_TOPK)

    def chunk(c):
        return slice(c * LANES, (c + 1) * LANES)

    def scores(hp):
        st_scr[...] = _dot_nt(keys_ref[hp], qp_ref[hp])

    def select_fast():
        def store_half(hp, c0, s, sv, member):
            sm = jnp.where(member, s, -jnp.inf)
            e = jnp.where(member, jnp.exp(s - sv[0:1]), 0.0)
            for k in range(s.shape[1] // LANES):
                sv_scr[hp, c0 + k] = sv[:, chunk(k)]
                sm_scr[hp, c0 + k] = sm[:, chunk(k)]
                e_scr[hp, c0 + k] = e[:, chunk(k)]

        def half_body(hp, bad):
            scores(hp)
            for w in range(c_tok // PEER_SELECT_LANES):
                c0 = w * (PEER_SELECT_LANES // LANES)
                s = st_scr[:, w * PEER_SELECT_LANES:(w + 1) * PEER_SELECT_LANES]
                sv = _top_values(s, PEER_TOPK)
                member = s >= sv[PEER_TOPK - 1:PEER_TOPK]
                n_mem = jnp.sum(jnp.where(member, 1.0, 0.0), axis=0, keepdims=True)
                store_half(hp, c0, s, sv, member)

                @pl.when(jnp.max(jnp.where(n_mem != k_top, 1.0, 0.0)) > 0.0)
                def _():
                    sv_x, rank = _extract_topk(s, PEER_TOPK)
                    store_half(hp, c0, s, sv_x, rank < k_top)
            return bad

        bad = lax.fori_loop(0, 2 * PEER_HEADS, half_body, jnp.zeros((1, LANES), F32))

        def head_body(h, bad):
            for c in range(n_chunks):
                cand = _staircase_sums(sv_scr[2 * h, c], sv_scr[2 * h + 1, c])
                tau, n_ge, z = _kth_largest(cand, PEER_TOPK)
                bad = jnp.maximum(bad, jnp.where(n_ge != k_top, 1.0, 0.0))
                tau_scr[h, c] = tau
                e_scr[2 * h, c] = e_scr[2 * h, c] / z
                sv0, sv1, s0m = sv_scr[2 * h, c], sv_scr[2 * h + 1, c], sm_scr[2 * h, c]
                theta = jnp.full(s0m.shape, jnp.inf, F32)
                for k0 in range(PEER_TOPK):
                    reach = jnp.where(sv0[k0:k0 + 1] + sv1 >= tau, sv1, jnp.inf)
                    theta = jnp.where(s0m == sv0[k0:k0 + 1], jnp.min(reach, axis=0, keepdims=True), theta)
                sm_scr[2 * h, c] = theta
            return bad

        return lax.fori_loop(0, PEER_HEADS, head_body, bad)

    def select_exact():
        def half_body(hp, carry):
            scores(hp)
            for c in range(n_chunks):
                s = st_scr[:, chunk(c)]
                sv, rank = _extract_topk(s, PEER_TOPK)
                member = rank < k_top
                sv_scr[hp, c] = sv
                sm_scr[hp, c] = jnp.where(member, s, -jnp.inf)
                e_scr[hp, c] = jnp.where(member, jnp.exp(s - sv[0:1]), 0.0)
                r_scr[hp, c] = rank
            return carry

        lax.fori_loop(0, 2 * PEER_HEADS, half_body, 0)

        def head_body(h, carry):
            for c in range(n_chunks):
                cand = _pair_sums(sv_scr[2 * h, c], sv_scr[2 * h + 1, c])
                fv, crank = _extract_topk(cand, PEER_TOPK)
                z = jnp.sum(jnp.exp(fv - fv[0:1]), axis=0, keepdims=True)
                pos = lax.broadcasted_iota(jnp.int32, cand.shape, 0).astype(F32)
                tau_scr[h, c] = fv[PEER_TOPK - 1:PEER_TOPK]
                ptau_scr[h, c] = jnp.sum(jnp.where(crank == k_top - 1.0, pos, 0.0), axis=0, keepdims=True)
                e_scr[2 * h, c] = e_scr[2 * h, c] / z
                r_scr[2 * h, c] = r_scr[2 * h, c] * k_top
            return carry

        lax.fori_loop(0, PEER_HEADS, head_body, 0)

    @pl.when(et == 0)
    def _():
        bad = select_fast()
        flag_ref[0] = jnp.max(bad).astype(jnp.int32)
        acc_scr[...] = jnp.zeros(acc_scr.shape, F32)

    @pl.when((et == 0) & (flag_ref[0] != 0))
    def _():
        select_exact()

    a_scr[...] = _dot_nt(u_ref[...], h2_ref[...])

    def weights(exact_ties):
        def body(ii, carry):
            i = et * i_per_tile + ii
            rows = pl.ds(pl.multiple_of(ii * PEER_NKEYS, PEER_NKEYS), PEER_NKEYS)
            for c in range(n_chunks):
                g = None
                for h in range(PEER_HEADS):
                    val = e_scr[2 * h + 1, c] * e_scr[2 * h, c, pl.ds(i, 1), :]
                    if exact_ties:
                        tsum = sm_scr[2 * h + 1, c] + sm_scr[2 * h, c, pl.ds(i, 1), :]
                        tau = tau_scr[h, c]
                        pos = r_scr[2 * h + 1, c] + r_scr[2 * h, c, pl.ds(i, 1), :]
                        keep = (tsum > tau) | ((tsum == tau) & (pos <= ptau_scr[h, c]))
                    else:
                        keep = sm_scr[2 * h + 1, c] >= sm_scr[2 * h, c, pl.ds(i, 1), :]
                    contrib = jnp.where(keep, val, 0.0)
                    g = contrib if g is None else g + contrib
                wt_scr[rows, chunk(c)] = (g * _gelu_tanh(a_scr[rows, chunk(c)])).astype(BF16)
            return carry

        lax.fori_loop(0, i_per_tile, body, 0)

    @pl.when(flag_ref[0] == 0)
    def _():
        weights(False)

    @pl.when(flag_ref[0] != 0)
    def _():
        weights(True)

    acc_scr[...] += jnp.dot(v_ref[...], wt_scr[...], preferred_element_type=F32)

    @pl.when(et == n_et - 1)
    def _():
        y_ref[...] = x1_ref[...] + mod_ref[...][:, 5:6, :] * acc_scr[...].T.reshape(nb, t, d)


def _peer(x1, mod, h2, qp, keys_bf, u_bf, vt_bf, nb, t):
    bt, tt, d = x1.shape
    g0, g1 = bt // nb, tt // t
    c_tok = nb * t
    n_chunks = c_tok // LANES
    n_exp = u_bf.shape[0]
    te = PEER_EXPERT_TILE
    n_et = n_exp // te
    n_hp = qp.shape[0]
    tok3 = pl.BlockSpec((nb, t, d), lambda i, j, e: (i, j, 0))
    halves = lambda r: pltpu.VMEM((n_hp, n_chunks, r, LANES), F32)
    heads = lambda r: pltpu.VMEM((PEER_HEADS, n_chunks, r, LANES), F32)
    return pl.pallas_call(
        _peer_kernel,
        grid=(g0, g1, n_et),
        in_specs=[
            tok3,
            pl.BlockSpec((nb, N_MOD, d), lambda i, j, e: (i, 0, 0)),
            pl.BlockSpec((c_tok, d), lambda i, j, e: (i * g1 + j, 0)),
            pl.BlockSpec((n_hp, c_tok, PEER_NKEYS), lambda i, j, e: (0, i * g1 + j, 0)),
            pl.BlockSpec(keys_bf.shape, lambda i, j, e: (0, 0, 0)),
            pl.BlockSpec((te, d), lambda i, j, e: (e, 0)),
            pl.BlockSpec((d, te), lambda i, j, e: (0, e)),
        ],
        out_specs=tok3,
        out_shape=jax.ShapeDtypeStruct((bt, tt, d), F32),
        scratch_shapes=[
            halves(PEER_NKEYS), halves(PEER_NKEYS), halves(PEER_NKEYS), halves(PEER_TOPK),
            heads(1), heads(1),
            pltpu.VMEM((PEER_NKEYS, c_tok), F32),
            pltpu.VMEM((te, c_tok), F32),
            pltpu.VMEM((te, c_tok), BF16),
            pltpu.VMEM((d, c_tok), F32),
            pltpu.SMEM((1,), jnp.int32),
        ],
        compiler_params=_cparams(3),
        name="peer",
    )(x1, mod, h2, qp, keys_bf, u_bf, vt_bf)


def _block_diag(w):
    n, c, d = w.shape
    eye = jnp.eye(n, dtype=w.dtype)
    return (eye[:, None, :, None] * w[:, :, None, :]).reshape(n * c, n * d)


def kernel(x_prompt, x_sample, cache_k, cache_v, state_conv, state_h, page_table, c_prompt, c_sample, w_ada, b_ada, norm_mix, norm_ff, w_in, q_gain, k_gain, conv_w, conv_b, lru_wa, lru_ba, lru_wx, lru_bx, lru_lambda, attn_out_gain, rnn_out_gain, w_out, peer_wq, peer_keys, peer_u, peer_v):
    depth = w_ada.shape[0]
    batch, seq, d_model = x_prompt.shape
    dec_batch, dec_seq, _ = x_sample.shape
    _, n_pool, page_size, n_heads, head_dim = cache_k.shape
    n_pages = page_table.shape[1]
    d_rnn = d_model - D_ATTN
    assert (n_heads, head_dim) == (N_HEADS, HEAD_DIM)
    assert seq % ROW_TILE == 0 and ROW_TILE % MOBA_BLOCK == 0 and ROW_TILE % dec_seq == 0
    assert dec_seq == SUBLANES and (dec_batch * dec_seq) % ROW_TILE == 0
    assert (n_pages * page_size) % MOBA_BLOCK == 0

    head_mean = _block_diag(jnp.full((N_HEADS, HEAD_DIM, HEAD_DIM), 1.0 / HEAD_DIM, F32)).astype(BF16)
    cache_k_pages = cache_k.transpose(0, 1, 3, 4, 2).reshape(depth * n_pool, D_ATTN, page_size)
    cache_v_pages = cache_v.transpose(0, 1, 3, 4, 2).reshape(depth * n_pool, D_ATTN, page_size)
    page_table_flat = page_table.reshape(-1).astype(jnp.int32)
    nb_s = ROW_TILE // dec_seq

    yp, ys = x_prompt, x_sample
    outs = [[] for _ in range(8)]
    for l in range(depth):
        mod = _ada(jnp.concatenate([c_prompt, c_sample], axis=0), w_ada[l], b_ada[l])
        mod = mod.reshape(batch + dec_batch, N_MOD, d_model)
        mod_p, mod_s = mod[:batch], mod[batch:]
        w_in_bf = w_in[l].astype(BF16)
        w_out_bf = w_out[l].astype(BF16)
        wq_bf = peer_wq[l].astype(BF16)
        keys_bf = peer_keys[l].reshape(2 * PEER_HEADS, PEER_NKEYS, -1).astype(BF16)
        u_bf = peer_u[l].astype(BF16)
        vt_bf = peer_v[l].T.astype(BF16)
        wa_bd = _block_diag(lru_wa[l]).astype(BF16)
        wx_bd = _block_diag(lru_wx[l]).astype(BF16)
        row = lambda a: a.reshape(1, -1)
        q_gain_t = row(jnp.tile(q_gain[l], N_HEADS))
        k_gain_t = row(jnp.tile(k_gain[l], N_HEADS))
        rglru_w = (conv_w[l], row(conv_b[l]), wa_bd, row(lru_ba[l]), wx_bd, row(lru_bx[l]),
                   row(lru_lambda[l]), row(rnn_out_gain[l]))

        q, k, v, xr, gr, kbf, vbf, ksum = _proj(yp, mod_p, row(norm_mix[l]), w_in_bf, head_mean, q_gain_t, k_gain_t,
                                                1, ROW_TILE, True)
        attn = _moba_prompt(q, kbf, vbf, ksum.reshape(batch, seq // MOBA_BLOCK, D_ATTN), batch, seq)
        rnn_n, conv_p, h_p = _rglru(xr.reshape(batch, seq, d_rnn), gr.reshape(batch, seq, d_rnn),
                                    jnp.zeros((batch, SUBLANES, d_rnn), F32), jnp.zeros((batch, 1, d_rnn), F32),
                                    *rglru_w, 1, ROW_TILE)
        x1, h2, qp = _mix(attn, rnn_n, yp, mod_p, row(attn_out_gain[l]), w_out_bf, row(norm_ff[l]), wq_bf, 1, ROW_TILE)
        yp = _peer(x1, mod_p, h2, qp, keys_bf, u_bf, vt_bf, 1, ROW_TILE)
        outs[0].append(k.reshape(batch, seq, N_HEADS, HEAD_DIM))
        outs[1].append(v.reshape(batch, seq, N_HEADS, HEAD_DIM))
        outs[2].append(conv_p[:, SUBLANES - (CONV_WIDTH - 1):])
        outs[3].append(h_p[:, SUBLANES - 1])

        q, k, v, xr, gr = _proj(ys, mod_s, row(norm_mix[l]), w_in_bf, head_mean, q_gain_t, k_gain_t,
                                nb_s, dec_seq, False)
        tok3 = lambda a: a.reshape(dec_batch, dec_seq, -1)
        attn = _moba_sample(tok3(q), tok3(k), tok3(v), cache_k_pages, cache_v_pages, page_table_flat,
                            l * n_pool, n_pages, page_size)
        buf0 = jnp.pad(state_conv[l], ((0, 0), (SUBLANES - (CONV_WIDTH - 1), 0), (0, 0)))
        rnn_n, conv_s, h_s = _rglru(tok3(xr), tok3(gr), buf0, state_h[l][:, None, :], *rglru_w, nb_s, dec_seq)
        x1, h2, qp = _mix(attn.reshape(dec_batch * dec_seq, D_ATTN), rnn_n, ys, mod_s, row(attn_out_gain[l]),
                          w_out_bf, row(norm_ff[l]), wq_bf, nb_s, dec_seq)
        ys = _peer(x1, mod_s, h2, qp, keys_bf, u_bf, vt_bf, nb_s, dec_seq)
        outs[4].append(k.reshape(dec_batch, dec_seq, N_HEADS, HEAD_DIM))
        outs[5].append(v.reshape(dec_batch, dec_seq, N_HEADS, HEAD_DIM))
        outs[6].append(conv_s[:, SUBLANES - (CONV_WIDTH - 1):])
        outs[7].append(h_s[:, SUBLANES - 1])
    return (yp, ys) + tuple(jnp.stack(o) for o in outs)
```
